```python
import jax, jax.numpy as jnp
from jax import lax
import numpy as np

D_MODEL = 1024
BATCH = 32
SEQ = 2048
DEPTH = 2

CHUNK = 64
D_MIX = D_MODEL
D_CONV = D_MIX // 4
CONV_WIDTH = 3
D_FOX = D_MIX // 2
FOX_HEAD_DIM = 64
N_FOX_HEADS = D_FOX // FOX_HEAD_DIM
FOX_BLOCK = 128
D_SGU = D_MIX - D_CONV - D_FOX
N_SGU_GROUPS = 4
SGU_GROUP_DIM = D_SGU // N_SGU_GROUPS
SGU_CHUNK = 128
D_FF = 2816
ALPHA = (2 * DEPTH) ** 0.25
BETA = (8 * DEPTH) ** -0.25
LN_EPS = 1e-5
SPLIT_SIZES = (D_CONV, D_CONV, D_CONV, D_FOX, D_FOX, D_FOX, N_FOX_HEADS, D_SGU, D_SGU)
D_IN = 3 * D_CONV + 3 * D_FOX + N_FOX_HEADS + 2 * D_SGU

kernel_name = "hybrid_conv_fox_sgu_deepnorm_macaron"


def layer_norm(x, g, b):
    xf = x.astype(jnp.float32)
    mu = jnp.mean(xf, axis=-1, keepdims=True)
    xc = xf - mu
    var = jnp.mean(xc * xc, axis=-1, keepdims=True)
    y = xc * lax.rsqrt(var + LN_EPS) * g.astype(jnp.float32) + b.astype(jnp.float32)
    return y.astype(x.dtype)


def swiglu(x, w_up, w_down):
    gate, up = jnp.split(x @ w_up, 2, axis=-1)
    return (jax.nn.silu(gate) * up) @ w_down


def short_conv(gate_b, gate_c, h, w_conv):
    z = gate_c * h
    y = lax.conv_general_dilated(
        z, w_conv[:, None, :].astype(z.dtype),
        window_strides=(1,), padding=[(CONV_WIDTH - 1, 0)],
        dimension_numbers=("NWC", "WIO", "NWC"),
        feature_group_count=D_CONV)
    return gate_b * y


def forgetting_attention(q, k, v, f_logit):
    seq = q.shape[1]
    scale = FOX_HEAD_DIM ** -0.5
    log_f = jax.nn.log_sigmoid(f_logit.astype(jnp.float32))
    cum = jnp.transpose(jnp.cumsum(log_f, axis=1), (0, 2, 1))
    qf = q.astype(jnp.float32) * scale
    kf = k.astype(jnp.float32)
    vf = v.astype(jnp.float32)
    outs = []
    for i in range(seq // FOX_BLOCK):
        lo, hi = i * FOX_BLOCK, (i + 1) * FOX_BLOCK
        s = jnp.einsum('bqhd,bkhd->bhqk', qf[:, lo:hi], kf[:, :hi])
        s = s + cum[:, :, lo:hi, None] - cum[:, :, None, :hi]
        mask = jnp.arange(hi)[None, :] <= jnp.arange(lo, hi)[:, None]
        p = jax.nn.softmax(jnp.where(mask, s, -jnp.inf), axis=-1)
        outs.append(jnp.einsum('bhqk,bkhd->bqhd', p, vf[:, :hi]))
    return jnp.concatenate(outs, axis=1).astype(v.dtype)


def spatial_gating(u, v, ln_g, ln_b, w_s, b_s):
    bsz, seq, _ = v.shape
    u = jax.nn.gelu(u)
    v = layer_norm(jax.nn.gelu(v), ln_g, ln_b)
    vg = v.reshape(bsz, seq // SGU_CHUNK, SGU_CHUNK, N_SGU_GROUPS, SGU_GROUP_DIM)
    causal = jnp.tril(jnp.ones((SGU_CHUNK, SGU_CHUNK), dtype=w_s.dtype))
    mixed = jnp.einsum('gts,bnsgc->bntgc', w_s * causal, vg) + jnp.transpose(b_s)[:, :, None]
    return u * mixed.reshape(bsz, seq, D_SGU)


def hybrid_mixer(x, w_in, b_f, w_conv, sgu_ln_g, sgu_ln_b, w_s, b_s, w_out):
    bsz, seq, _ = x.shape
    offsets = []
    acc = 0
    for n in SPLIT_SIZES[:-1]:
        acc += n
        offsets.append(acc)
    proj = x @ w_in
    cb, cc, ch, q, k, v, f_logit, su, sv = jnp.split(proj, offsets, axis=-1)
    y_a = short_conv(cb, cc, ch, w_conv)
    heads = (bsz, seq, N_FOX_HEADS, FOX_HEAD_DIM)
    y_b = forgetting_attention(q.reshape(heads), k.reshape(heads), v.reshape(heads),
                               f_logit + b_f).reshape(bsz, seq, D_FOX)
    y_c = spatial_gating(su, sv, sgu_ln_g, sgu_ln_b, w_s, b_s)
    return jnp.concatenate([y_a, y_b, y_c], axis=-1) @ w_out


def setup_inputs(seed: int = 0) -> dict:
    key = jax.random.key(seed)
    ks = jax.random.split(key, 20)
    nrm = lambda k, shape, s: jax.random.normal(k, shape, jnp.float32) * s
    L, D = DEPTH, D_MODEL
    return {
        "x": nrm(ks[0], (BATCH, SEQ, D), 1.0),
        "ln1_g": 1.0 + nrm(ks[1], (L, D), 0.05),
        "ln1_b": nrm(ks[2], (L, D), 0.02),
        "ffn1_w_up": nrm(ks[3], (L, D, 2 * D_FF), D ** -0.5),
        "ffn1_w_down": nrm(ks[4], (L, D_FF, D), BETA * D_FF ** -0.5),
        "mix_w_in": nrm(ks[5], (L, D, D_IN), D ** -0.5),
        "fox_b_f": 4.0 + nrm(ks[6], (L, N_FOX_HEADS), 0.5),
        "conv_w": nrm(ks[7], (L, CONV_WIDTH, D_CONV), CONV_WIDTH ** -0.5),
        "sgu_ln_g": 1.0 + nrm(ks[8], (L, D_SGU), 0.05),
        "sgu_ln_b": nrm(ks[9], (L, D_SGU), 0.02),
        "sgu_w_s": nrm(ks[10], (L, N_SGU_GROUPS, SGU_CHUNK, SGU_CHUNK), SGU_CHUNK ** -0.5),
        "sgu_b_s": 1.0 + nrm(ks[11], (L, N_SGU_GROUPS, SGU_CHUNK), 0.02),
        "mix_w_out": nrm(ks[12], (L, D_MIX, D), BETA * D_MIX ** -0.5),
        "ln2_g": 1.0 + nrm(ks[13], (L, D), 0.05),
        "ln2_b": nrm(ks[14], (L, D), 0.02),
        "ffn2_w_up": nrm(ks[15], (L, D, 2 * D_FF), D ** -0.5),
        "ffn2_w_down": nrm(ks[16], (L, D_FF, D), BETA * D_FF ** -0.5),
        "ln3_g": 1.0 + nrm(ks[17], (L, D), 0.05),
        "ln3_b": nrm(ks[18], (L, D), 0.02),
    }


def reference(x, ln1_g, ln1_b, ffn1_w_up, ffn1_w_down, mix_w_in, fox_b_f, conv_w,
              sgu_ln_g, sgu_ln_b, sgu_w_s, sgu_b_s, mix_w_out, ln2_g, ln2_b,
              ffn2_w_up, ffn2_w_down, ln3_g, ln3_b):
    for l in range(DEPTH):
        x = layer_norm(ALPHA * x + 0.5 * swiglu(x, ffn1_w_up[l], ffn1_w_down[l]), ln1_g[l], ln1_b[l])
        mix = hybrid_mixer(x, mix_w_in[l], fox_b_f[l], conv_w[l], sgu_ln_g[l], sgu_ln_b[l],
                           sgu_w_s[l], sgu_b_s[l], mix_w_out[l])
        x = layer_norm(ALPHA * x + mix, ln2_g[l], ln2_b[l])
        x = layer_norm(ALPHA * x + 0.5 * swiglu(x, ffn2_w_up[l], ffn2_w_down[l]), ln3_g[l], ln3_b[l])
    return x
```

```python
import functools

import jax
import jax.numpy as jnp
from jax import lax
from jax.experimental import pallas as pl
from jax.experimental.pallas import tpu as pltpu

D_MODEL = 1024
DEPTH = 2
D_CONV = 256
CONV_WIDTH = 3
D_FOX = 512
FOX_HEAD_DIM = 64
N_FOX_HEADS = 8
D_SGU = 256
N_SGU_GROUPS = 4
SGU_GROUP_DIM = 64
SGU_CHUNK = 128
D_FF = 2816
ALPHA = (2 * DEPTH) ** 0.25
LN_EPS = 1e-5

LANES = 128
CONV_HALO = 8
VMEM_LIMIT_BYTES = 56 * 1024 * 1024

ROW_TILE = 512
FF_CHUNK = 256
ATTN_BLOCK = 256


def _layer_norm(h, g, b):
    mu = jnp.mean(h, axis=-1, keepdims=True)
    hc = h - mu
    var = jnp.mean(hc * hc, axis=-1, keepdims=True)
    return hc * lax.rsqrt(var + LN_EPS) * g + b


def _const_spec(shape):
    return pl.BlockSpec(shape, lambda *_: (0,) * len(shape), pipeline_mode=pl.Buffered(1))


def _ffn_ln_kernel(x_ref, wup_ref, wdn_ref, g_ref, b_ref, o_ref):
    x = x_ref[...]
    xb = x.astype(jnp.bfloat16)
    acc = jnp.zeros(x.shape, jnp.float32)
    for c in range(D_FF // FF_CHUNK):
        lo = c * FF_CHUNK
        gate = jnp.dot(xb, wup_ref[:, lo:lo + FF_CHUNK], preferred_element_type=jnp.float32)
        up = jnp.dot(xb, wup_ref[:, D_FF + lo:D_FF + lo + FF_CHUNK], preferred_element_type=jnp.float32)
        act = (jax.nn.silu(gate) * up).astype(jnp.bfloat16)
        acc = acc + jnp.dot(act, wdn_ref[lo:lo + FF_CHUNK, :], preferred_element_type=jnp.float32)
    o_ref[...] = _layer_norm(ALPHA * x + 0.5 * acc, g_ref[...], b_ref[...])


def _ffn_ln(x2d, w_up, w_down, g, b):
    rows = x2d.shape[0]
    return pl.pallas_call(
        _ffn_ln_kernel,
        grid=(rows // ROW_TILE,),
        in_specs=[
            pl.BlockSpec((ROW_TILE, D_MODEL), lambda i: (i, 0)),
            _const_spec((D_MODEL, 2 * D_FF)),
            _const_spec((D_FF, D_MODEL)),
            _const_spec((1, D_MODEL)),
            _const_spec((1, D_MODEL)),
        ],
        out_specs=pl.BlockSpec((ROW_TILE, D_MODEL), lambda i: (i, 0)),
        out_shape=jax.ShapeDtypeStruct((rows, D_MODEL), jnp.float32),
        compiler_params=pltpu.CompilerParams(
            dimension_semantics=("arbitrary",), vmem_limit_bytes=VMEM_LIMIT_BYTES),
        name="ffn_ln",
    )(x2d, w_up, w_down, g, b)


def _log_sigmoid(x):
    return -(jnp.maximum(-x, 0.0) + jnp.log1p(jnp.exp(-jnp.abs(x))))


def _cumsum_rows(y):
    n = y.shape[0]
    row = lax.broadcasted_iota(jnp.int32, y.shape, 0)
    shift = 1
    while shift < n:
        y = y + jnp.where(row >= shift, pltpu.roll(y, shift, axis=0), 0.0)
        shift *= 2
    return y


def _mixer_in_kernel(x_ref, wc_ref, wqkv_ref, ws_ref, wf_ref, bf_ref, convw_ref, sg_ref, sb_ref,
                     wsp_ref, bsp_ref,
                     ya_ref, yc_ref, q_ref, k_ref, v_ref, cum_ref, cumt_ref,
                     zbuf, fcarry):
    j = pl.program_id(1)
    tm = x_ref.shape[1]

    @pl.when(j == 0)
    def _():
        zbuf[0:CONV_HALO, :] = jnp.zeros((CONV_HALO, D_CONV), jnp.float32)
        fcarry[...] = jnp.zeros(fcarry.shape, jnp.float32)

    xb = x_ref[0].astype(jnp.bfloat16)

    pc = jnp.dot(xb, wc_ref[...], preferred_element_type=jnp.float32)
    gate_b = pc[:, 0:D_CONV]
    z = pc[:, D_CONV:2 * D_CONV] * pc[:, 2 * D_CONV:3 * D_CONV]
    zbuf[CONV_HALO:CONV_HALO + tm, :] = z
    z1 = zbuf[CONV_HALO - 1:CONV_HALO - 1 + tm, :]
    z2 = zbuf[CONV_HALO - 2:CONV_HALO - 2 + tm, :]
    w = convw_ref[...]
    conv = w[0:1, :] * z2 + w[1:2, :] * z1 + w[2:3, :] * z
    ya_ref[0] = (gate_b * conv).astype(ya_ref.dtype)
    zbuf[0:CONV_HALO, :] = z[tm - CONV_HALO:tm, :]

    pqkv = jnp.dot(xb, wqkv_ref[...], preferred_element_type=jnp.float32)
    q_ref[0] = (pqkv[:, 0:D_FOX] * (FOX_HEAD_DIM ** -0.5)).astype(q_ref.dtype)
    k_ref[0] = pqkv[:, D_FOX:2 * D_FOX].astype(k_ref.dtype)
    v_ref[0] = pqkv[:, 2 * D_FOX:3 * D_FOX].astype(v_ref.dtype)
    f_logit = jnp.dot(xb, wf_ref[...], preferred_element_type=jnp.float32) + bf_ref[...]
    cum = _cumsum_rows(_log_sigmoid(f_logit)) + fcarry[...]
    fcarry[...] = cum[tm - 1:tm, :]
    cum_ref[0] = cum
    cumt_ref[0] = cum.T[0:N_FOX_HEADS, :]

    ps = jnp.dot(xb, ws_ref[...], preferred_element_type=jnp.float32)
    u = jax.nn.gelu(ps[:, 0:D_SGU])
    vn = _layer_norm(jax.nn.gelu(ps[:, D_SGU:2 * D_SGU]), sg_ref[...], sb_ref[...]).astype(jnp.bfloat16)
    r = lax.broadcasted_iota(jnp.int32, (SGU_CHUNK, SGU_CHUNK), 0)
    c = lax.broadcasted_iota(jnp.int32, (SGU_CHUNK, SGU_CHUNK), 1)
    causal = c <= r
    lane = lax.broadcasted_iota(jnp.int32, (SGU_CHUNK, LANES), 1)
    first_group = lane < SGU_GROUP_DIM
    bias = bsp_ref[...]
    for half in range(D_SGU // LANES):
        w_pair = jnp.concatenate(
            [jnp.where(causal, wsp_ref[2 * half], 0.0), jnp.where(causal, wsp_ref[2 * half + 1], 0.0)],
            axis=0).astype(jnp.bfloat16)
        for n in range(tm // SGU_CHUNK):
            rows = slice(n * SGU_CHUNK, (n + 1) * SGU_CHUNK)
            cols = slice(half * LANES, (half + 1) * LANES)
            both = jnp.dot(w_pair, vn[rows, cols], preferred_element_type=jnp.float32)
            mixed = jnp.where(first_group, both[0:SGU_CHUNK], both[SGU_CHUNK:2 * SGU_CHUNK]) + bias[:, cols]
            yc_ref[0, rows, cols] = (u[rows, cols] * mixed).astype(yc_ref.dtype)


def _mixer_in(x3d, wc, wqkv, ws, wf, bf, conv_w, sg, sb, w_sp, b_sp):
    bsz, seq, _ = x3d.shape
    tm = ROW_TILE
    row_block = lambda width: pl.BlockSpec((1, tm, width), lambda b, j: (b, j, 0))
    bf16 = jnp.bfloat16
    return pl.pallas_call(
        _mixer_in_kernel,
        grid=(bsz, seq // tm),
        in_specs=[
            row_block(D_MODEL),
            _const_spec(wc.shape), _const_spec(wqkv.shape), _const_spec(ws.shape), _const_spec(wf.shape),
            _const_spec(bf.shape), _const_spec(conv_w.shape), _const_spec(sg.shape), _const_spec(sb.shape),
            _const_spec(w_sp.shape), _const_spec(b_sp.shape),
        ],
        out_specs=[
            row_block(D_CONV), row_block(D_SGU), row_block(D_FOX), row_block(D_FOX), row_block(D_FOX),
            row_block(LANES),
            pl.BlockSpec((1, N_FOX_HEADS, tm), lambda b, j: (b, 0, j)),
        ],
        out_shape=[
            jax.ShapeDtypeStruct((bsz, seq, D_CONV), bf16),
            jax.ShapeDtypeStruct((bsz, seq, D_SGU), bf16),
            jax.ShapeDtypeStruct((bsz, seq, D_FOX), bf16),
            jax.ShapeDtypeStruct((bsz, seq, D_FOX), bf16),
            jax.ShapeDtypeStruct((bsz, seq, D_FOX), bf16),
            jax.ShapeDtypeStruct((bsz, seq, LANES), jnp.float32),
            jax.ShapeDtypeStruct((bsz, N_FOX_HEADS, seq), jnp.float32),
        ],
        scratch_shapes=[
            pltpu.VMEM((CONV_HALO + tm, D_CONV), jnp.float32),
            pltpu.VMEM((1, LANES), jnp.float32),
        ],
        compiler_params=pltpu.CompilerParams(
            dimension_semantics=("arbitrary", "arbitrary"), vmem_limit_bytes=VMEM_LIMIT_BYTES),
        name="mixer_in",
    )(x3d, wc, wqkv, ws, wf, bf, conv_w, sg, sb, w_sp, b_sp)


def _fox_attn_kernel(q_ref, k_ref, v_ref, cum_ref, cumt_ref, o_ref, m_sc, l_sc, acc_sc):
    hp = pl.program_id(1)
    i = pl.program_id(2)
    tq = q_ref.shape[1]
    tk = tq
    lane = lax.broadcasted_iota(jnp.int32, (tq, LANES), 1)
    first_head = lane < FOX_HEAD_DIM

    q2 = q_ref[0]
    zero = jnp.zeros_like(q2)
    q_stack = jnp.concatenate([jnp.where(first_head, q2, zero), jnp.where(first_head, zero, q2)], axis=0)

    cum_q = cum_ref[0]
    pick = lambda h: jnp.sum(jnp.where(lane == h, cum_q, 0.0), axis=-1, keepdims=True)
    cq = jnp.concatenate([pick(2 * hp), pick(2 * hp + 1)], axis=0)

    m_sc[...] = jnp.full(m_sc.shape, -jnp.inf, jnp.float32)
    l_sc[...] = jnp.zeros(l_sc.shape, jnp.float32)
    acc_sc[...] = jnp.zeros(acc_sc.shape, jnp.float32)

    def step(j, diagonal):
        start = pl.multiple_of(j * tk, tk)
        k2 = k_ref[0, pl.ds(start, tk), :]
        v2 = v_ref[0, pl.ds(start, tk), :]
        s = lax.dot_general(q_stack, k2, (((1,), (1,)), ((), ())), preferred_element_type=jnp.float32)
        ck0 = cumt_ref[0, pl.ds(2 * hp, 1), pl.ds(start, tk)]
        ck1 = cumt_ref[0, pl.ds(2 * hp + 1, 1), pl.ds(start, tk)]
        ck = jnp.concatenate([jnp.broadcast_to(ck0, (tq, tk)), jnp.broadcast_to(ck1, (tq, tk))], axis=0)
        s = s + cq - ck
        if diagonal:
            r = lax.broadcasted_iota(jnp.int32, (tq, tk), 0)
            c = lax.broadcasted_iota(jnp.int32, (tq, tk), 1)
            keep = jnp.concatenate([c <= r, c <= r], axis=0)
            s = jnp.where(keep, s, -jnp.inf)
        m_old = m_sc[...]
        m_new = jnp.maximum(m_old, jnp.max(s, axis=-1, keepdims=True))
        scale = jnp.exp(m_old - m_new)
        p = jnp.exp(s - m_new)
        l_sc[...] = scale * l_sc[...] + jnp.sum(p, axis=-1, keepdims=True)
        acc_sc[...] = scale * acc_sc[...] + jnp.dot(p.astype(jnp.bfloat16), v2,
                                                    preferred_element_type=jnp.float32)
        m_sc[...] = m_new

    def body(j, carry):
        step(j, diagonal=False)
        return carry

    lax.fori_loop(0, i, body, 0)
    step(i, diagonal=True)

    o = acc_sc[...] / l_sc[...]
    o_ref[0] = jnp.where(first_head, o[0:tq], o[tq:2 * tq]).astype(o_ref.dtype)


def _fox_attn(q, k, v, cum, cumt):
    bsz, seq, _ = q.shape
    tq = ATTN_BLOCK
    n_pairs = D_FOX // LANES
    return pl.pallas_call(
        _fox_attn_kernel,
        grid=(bsz, n_pairs, seq // tq),
        in_specs=[
            pl.BlockSpec((1, tq, LANES), lambda b, h, i: (b, i, h)),
            pl.BlockSpec((1, seq, LANES), lambda b, h, i: (b, 0, h)),
            pl.BlockSpec((1, seq, LANES), lambda b, h, i: (b, 0, h)),
            pl.BlockSpec((1, tq, LANES), lambda b, h, i: (b, i, 0)),
            pl.BlockSpec((1, N_FOX_HEADS, seq), lambda b, h, i: (b, 0, 0)),
        ],
        out_specs=pl.BlockSpec((1, tq, LANES), lambda b, h, i: (b, i, h)),
        out_shape=jax.ShapeDtypeStruct((bsz, seq, D_FOX), jnp.bfloat16),
        scratch_shapes=[
            pltpu.VMEM((2 * tq, 1), jnp.float32),
            pltpu.VMEM((2 * tq, 1), jnp.float32),
            pltpu.VMEM((2 * tq, LANES), jnp.float32),
        ],
        compiler_params=pltpu.CompilerParams(
            dimension_semantics=("arbitrary", "arbitrary", "arbitrary"), vmem_limit_bytes=VMEM_LIMIT_BYTES),
        name="fox_attn",
    )(q, k, v, cum, cumt)


def _mixer_out_kernel(x_ref, ya_ref, yb_ref, yc_ref, wa_ref, wb_ref, wc_ref, g_ref, b_ref, o_ref):
    mix = jnp.dot(ya_ref[...], wa_ref[...], preferred_element_type=jnp.float32)
    mix = mix + jnp.dot(yb_ref[...], wb_ref[...], preferred_element_type=jnp.float32)
    mix = mix + jnp.dot(yc_ref[...], wc_ref[...], preferred_element_type=jnp.float32)
    o_ref[...] = _layer_norm(ALPHA * x_ref[...] + mix, g_ref[...], b_ref[...])


def _mixer_out(x2d, ya, yb, yc, wa, wb, wc, g, b):
    rows = x2d.shape[0]
    row_block = lambda width: pl.BlockSpec((ROW_TILE, width), lambda i: (i, 0))
    return pl.pallas_call(
        _mixer_out_kernel,
        grid=(rows // ROW_TILE,),
        in_specs=[
            row_block(D_MODEL), row_block(D_CONV), row_block(D_FOX), row_block(D_SGU),
            _const_spec(wa.shape), _const_spec(wb.shape), _const_spec(wc.shape),
            _const_spec((1, D_MODEL)), _const_spec((1, D_MODEL)),
        ],
        out_specs=row_block(D_MODEL),
        out_shape=jax.ShapeDtypeStruct((rows, D_MODEL), jnp.float32),
        compiler_params=pltpu.CompilerParams(
            dimension_semantics=("arbitrary",), vmem_limit_bytes=VMEM_LIMIT_BYTES),
        name="mixer_out",
    )(x2d, ya, yb, yc, wa, wb, wc, g, b)


def kernel(x, ln1_g, ln1_b, ffn1_w_up, ffn1_w_down, mix_w_in, fox_b_f, conv_w, sgu_ln_g, sgu_ln_b, sgu_w_s,
           sgu_b_s, mix_w_out, ln2_g, ln2_b, ffn2_w_up, ffn2_w_down, ln3_g, ln3_b):
    bsz, seq, d = x.shape
    assert d == D_MODEL and seq % ROW_TILE == 0 and seq % ATTN_BLOCK == 0
    bf16 = jnp.bfloat16
    rows = bsz * seq
    row_vec = lambda p: p.reshape(1, -1)
    o_qkv = 3 * D_CONV
    o_f = o_qkv + 3 * D_FOX
    o_s = o_f + N_FOX_HEADS
    pad_f = LANES - N_FOX_HEADS

    h = x.reshape(rows, d)
    for l in range(DEPTH):
        h = _ffn_ln(h, ffn1_w_up[l].astype(bf16), ffn1_w_down[l].astype(bf16), row_vec(ln1_g[l]), row_vec(ln1_b[l]))

        w_in = mix_w_in[l]
        ya, yc, q, k, v, cum, cumt = _mixer_in(
            h.reshape(bsz, seq, d),
            w_in[:, 0:o_qkv].astype(bf16),
            w_in[:, o_qkv:o_f].astype(bf16),
            w_in[:, o_s:o_s + 2 * D_SGU].astype(bf16),
            jnp.pad(w_in[:, o_f:o_s], ((0, 0), (0, pad_f))).astype(bf16),
            jnp.pad(fox_b_f[l], (0, pad_f)).reshape(1, LANES),
            conv_w[l], row_vec(sgu_ln_g[l]), row_vec(sgu_ln_b[l]),
            sgu_w_s[l],
            jnp.repeat(sgu_b_s[l].T, SGU_GROUP_DIM, axis=1))
        yb = _fox_attn(q, k, v, cum, cumt)

        w_out = mix_w_out[l].astype(bf16)
        h = _mixer_out(
            h, ya.reshape(rows, D_CONV), yb.reshape(rows, D_FOX), yc.reshape(rows, D_SGU),
            w_out[0:D_CONV], w_out[D_CONV:D_CONV + D_FOX], w_out[D_CONV + D_FOX:],
            row_vec(ln2_g[l]), row_vec(ln2_b[l]))

        h = _ffn_ln(h, ffn2_w_up[l].astype(bf16), ffn2_w_down[l].astype(bf16), row_vec(ln3_g[l]), row_vec(ln3_b[l]))
    return h.reshape(bsz, seq, d)
```

```python
import functools

import jax
import jax.numpy as jnp
from jax import lax
from jax.experimental import pallas as pl
from jax.experimental.pallas import tpu as pltpu

D_MODEL = 1024
DEPTH = 2
D_CONV = 256
CONV_WIDTH = 3
D_FOX = 512
FOX_HEAD_DIM = 64
N_FOX_HEADS = 8
D_SGU = 256
N_SGU_GROUPS = 4
SGU_GROUP_DIM = 64
SGU_CHUNK = 128
D_FF = 2816
ALPHA = (2 * DEPTH) ** 0.25
LN_EPS = 1e-5

LANES = 128
CONV_HALO = 8
VMEM_LIMIT_BYTES = 56 * 1024 * 1024

ROW_TILE = 512
FF_CHUNK = 256
ATTN_BLOCK = 512


def _layer_norm(h, g, b):
    mu = jnp.mean(h, axis=-1, keepdims=True)
    hc = h - mu
    var = jnp.mean(hc * hc, axis=-1, keepdims=True)
    return hc * lax.rsqrt(var + LN_EPS) * g + b


def _const_spec(shape):
    return pl.BlockSpec(shape, lambda *_: (0,) * len(shape), pipeline_mode=pl.Buffered(1))


def _ffn_ln_kernel(x_ref, wup_ref, wdn_ref, g_ref, b_ref, o_ref):
    x = x_ref[...]
    xb = x.astype(jnp.bfloat16)
    acc = jnp.zeros(x.shape, jnp.float32)
    for c in range(D_FF // FF_CHUNK):
        lo = c * FF_CHUNK
        gate = jnp.dot(xb, wup_ref[:, lo:lo + FF_CHUNK], preferred_element_type=jnp.float32)
        up = jnp.dot(xb, wup_ref[:, D_FF + lo:D_FF + lo + FF_CHUNK], preferred_element_type=jnp.float32)
        act = (jax.nn.silu(gate) * up).astype(jnp.bfloat16)
        acc = acc + jnp.dot(act, wdn_ref[lo:lo + FF_CHUNK, :], preferred_element_type=jnp.float32)
    o_ref[...] = _layer_norm(ALPHA * x + 0.5 * acc, g_ref[...], b_ref[...])


def _ffn_ln(x2d, w_up, w_down, g, b):
    rows = x2d.shape[0]
    return pl.pallas_call(
        _ffn_ln_kernel,
        grid=(rows // ROW_TILE,),
        in_specs=[
            pl.BlockSpec((ROW_TILE, D_MODEL), lambda i: (i, 0)),
            _const_spec((D_MODEL, 2 * D_FF)),
            _const_spec((D_FF, D_MODEL)),
            _const_spec((1, D_MODEL)),
            _const_spec((1, D_MODEL)),
        ],
        out_specs=pl.BlockSpec((ROW_TILE, D_MODEL), lambda i: (i, 0)),
        out_shape=jax.ShapeDtypeStruct((rows, D_MODEL), jnp.float32),
        compiler_params=pltpu.CompilerParams(
            dimension_semantics=("arbitrary",), vmem_limit_bytes=VMEM_LIMIT_BYTES),
        name="ffn_ln",
    )(x2d, w_up, w_down, g, b)


def _log_sigmoid(x):
    return -(jnp.maximum(-x, 0.0) + jnp.log1p(jnp.exp(-jnp.abs(x))))


def _cumsum_rows(y):
    n = y.shape[0]
    row = lax.broadcasted_iota(jnp.int32, y.shape, 0)
    shift = 1
    while shift < n:
        y = y + jnp.where(row >= shift, pltpu.roll(y, shift, axis=0), 0.0)
        shift *= 2
    return y


def _mixer_in_kernel(x_ref, wc_ref, wqkv_ref, ws_ref, wf_ref, bf_ref, convw_ref, sg_ref, sb_ref,
                     wsp_ref, bsp_ref,
                     ya_ref, yc_ref, q_ref, k_ref, v_ref, cum_ref, kb_ref,
                     zbuf, fcarry):
    j = pl.program_id(1)
    tm = x_ref.shape[1]

    @pl.when(j == 0)
    def _():
        zbuf[0:CONV_HALO, :] = jnp.zeros((CONV_HALO, D_CONV), jnp.float32)
        fcarry[...] = jnp.zeros(fcarry.shape, jnp.float32)

    xb = x_ref[0].astype(jnp.bfloat16)

    pc = jnp.dot(xb, wc_ref[...], preferred_element_type=jnp.float32)
    gate_b = pc[:, 0:D_CONV]
    z = pc[:, D_CONV:2 * D_CONV] * pc[:, 2 * D_CONV:3 * D_CONV]
    zbuf[CONV_HALO:CONV_HALO + tm, :] = z
    z1 = zbuf[CONV_HALO - 1:CONV_HALO - 1 + tm, :]
    z2 = zbuf[CONV_HALO - 2:CONV_HALO - 2 + tm, :]
    w = convw_ref[...]
    conv = w[0:1, :] * z2 + w[1:2, :] * z1 + w[2:3, :] * z
    ya_ref[0] = (gate_b * conv).astype(ya_ref.dtype)
    zbuf[0:CONV_HALO, :] = z[tm - CONV_HALO:tm, :]

    pqkv = jnp.dot(xb, wqkv_ref[...], preferred_element_type=jnp.float32)
    q_ref[0] = (pqkv[:, 0:D_FOX] * (FOX_HEAD_DIM ** -0.5)).astype(q_ref.dtype)
    k_ref[0] = pqkv[:, D_FOX:2 * D_FOX].astype(k_ref.dtype)
    v_ref[0] = pqkv[:, 2 * D_FOX:3 * D_FOX].astype(v_ref.dtype)
    f_logit = jnp.dot(xb, wf_ref[...], preferred_element_type=jnp.float32) + bf_ref[...]
    cum = _cumsum_rows(_log_sigmoid(f_logit)) + fcarry[...]
    fcarry[...] = cum[tm - 1:tm, :]
    cum_ref[0] = cum
    lane = lax.broadcasted_iota(jnp.int32, cum.shape, 1)
    c0 = jnp.where(lane < N_FOX_HEADS, cum, 0.0)
    hi = c0.astype(jnp.bfloat16).astype(jnp.float32)
    r1 = c0 - hi
    mid = r1.astype(jnp.bfloat16).astype(jnp.float32)
    lo = (r1 - mid).astype(jnp.bfloat16).astype(jnp.float32)
    pieces = hi + pltpu.roll(mid, N_FOX_HEADS, axis=1) + pltpu.roll(lo, 2 * N_FOX_HEADS, axis=1)
    kb_ref[0] = (-pieces).astype(kb_ref.dtype)

    ps = jnp.dot(xb, ws_ref[...], preferred_element_type=jnp.float32)
    u = jax.nn.gelu(ps[:, 0:D_SGU])
    vn = _layer_norm(jax.nn.gelu(ps[:, D_SGU:2 * D_SGU]), sg_ref[...], sb_ref[...]).astype(jnp.bfloat16)
    r = lax.broadcasted_iota(jnp.int32, (SGU_CHUNK, SGU_CHUNK), 0)
    c = lax.broadcasted_iota(jnp.int32, (SGU_CHUNK, SGU_CHUNK), 1)
    causal = c <= r
    lane = lax.broadcasted_iota(jnp.int32, (SGU_CHUNK, LANES), 1)
    first_group = lane < SGU_GROUP_DIM
    bias = bsp_ref[...]
    for half in range(D_SGU // LANES):
        w_pair = jnp.concatenate(
            [jnp.where(causal, wsp_ref[2 * half], 0.0), jnp.where(causal, wsp_ref[2 * half + 1], 0.0)],
            axis=0).astype(jnp.bfloat16)
        for n in range(tm // SGU_CHUNK):
            rows = slice(n * SGU_CHUNK, (n + 1) * SGU_CHUNK)
            cols = slice(half * LANES, (half + 1) * LANES)
            both = jnp.dot(w_pair, vn[rows, cols], preferred_element_type=jnp.float32)
            mixed = jnp.where(first_group, both[0:SGU_CHUNK], both[SGU_CHUNK:2 * SGU_CHUNK]) + bias[:, cols]
            yc_ref[0, rows, cols] = (u[rows, cols] * mixed).astype(yc_ref.dtype)


def _mixer_in(x3d, wc, wqkv, ws, wf, bf, conv_w, sg, sb, w_sp, b_sp):
    bsz, seq, _ = x3d.shape
    tm = ROW_TILE
    row_block = lambda width: pl.BlockSpec((1, tm, width), lambda b, j: (b, j, 0))
    bf16 = jnp.bfloat16
    return pl.pallas_call(
        _mixer_in_kernel,
        grid=(bsz, seq // tm),
        in_specs=[
            row_block(D_MODEL),
            _const_spec(wc.shape), _const_spec(wqkv.shape), _const_spec(ws.shape), _const_spec(wf.shape),
            _const_spec(bf.shape), _const_spec(conv_w.shape), _const_spec(sg.shape), _const_spec(sb.shape),
            _const_spec(w_sp.shape), _const_spec(b_sp.shape),
        ],
        out_specs=[
            row_block(D_CONV), row_block(D_SGU), row_block(D_FOX), row_block(D_FOX), row_block(D_FOX),
            row_block(LANES), row_block(LANES),
        ],
        out_shape=[
            jax.ShapeDtypeStruct((bsz, seq, D_CONV), bf16),
            jax.ShapeDtypeStruct((bsz, seq, D_SGU), bf16),
            jax.ShapeDtypeStruct((bsz, seq, D_FOX), bf16),
            jax.ShapeDtypeStruct((bsz, seq, D_FOX), bf16),
            jax.ShapeDtypeStruct((bsz, seq, D_FOX), bf16),
            jax.ShapeDtypeStruct((bsz, seq, LANES), jnp.float32),
            jax.ShapeDtypeStruct((bsz, seq, LANES), bf16),
        ],
        scratch_shapes=[
            pltpu.VMEM((CONV_HALO + tm, D_CONV), jnp.float32),
            pltpu.VMEM((1, LANES), jnp.float32),
        ],
        compiler_params=pltpu.CompilerParams(
            dimension_semantics=("arbitrary", "arbitrary"), vmem_limit_bytes=VMEM_LIMIT_BYTES),
        name="mixer_in",
    )(x3d, wc, wqkv, ws, wf, bf, conv_w, sg, sb, w_sp, b_sp)


N_PAIRS = D_FOX // LANES


def _fox_attn_kernel(q_ref, k_ref, v_ref, cum_ref, kb_ref, o_ref, qs_sc, cq_sc, m_sc, l_sc, acc_sc):
    i = pl.program_id(1)
    tq = q_ref.shape[1]
    tk = tq
    n_chunks = tk // LANES
    lane = lax.broadcasted_iota(jnp.int32, (tq, LANES), 1)
    first_head = lane < FOX_HEAD_DIM

    cum_q = cum_ref[0]
    for p in range(N_PAIRS):
        q2 = q_ref[0, :, p * LANES:(p + 1) * LANES]
        zero = jnp.zeros_like(q2)
        qs_sc[p, 0:tq, 0:LANES] = jnp.where(first_head, q2, zero)
        qs_sc[p, tq:2 * tq, 0:LANES] = jnp.where(first_head, zero, q2)
        for hh in range(2):
            h = 2 * p + hh
            cq_sc[p, hh * tq:(hh + 1) * tq, :] = jnp.broadcast_to(cum_q[:, h:h + 1], (tq, LANES))
            pick = (lane == h) | (lane == N_FOX_HEADS + h) | (lane == 2 * N_FOX_HEADS + h)
            qs_sc[p, hh * tq:(hh + 1) * tq, LANES:2 * LANES] = jnp.where(pick, 1.0, 0.0).astype(qs_sc.dtype)
    m_sc[...] = jnp.full(m_sc.shape, -jnp.inf, jnp.float32)
    l_sc[...] = jnp.zeros(l_sc.shape, jnp.float32)
    acc_sc[...] = jnp.zeros(acc_sc.shape, jnp.float32)

    row = lax.broadcasted_iota(jnp.int32, (2 * tq, LANES), 0)
    row = jnp.where(row >= tq, row - tq, row)
    col = lax.broadcasted_iota(jnp.int32, (2 * tq, LANES), 1)

    def step(j, diagonal):
        start = pl.multiple_of(j * tk, tk)
        for p in range(N_PAIRS):
            k2 = jnp.concatenate(
                [k_ref[0, pl.ds(start, tk), p * LANES:(p + 1) * LANES], kb_ref[0, pl.ds(start, tk), :]], axis=1)
            v2 = v_ref[0, pl.ds(start, tk), p * LANES:(p + 1) * LANES]
            s = lax.dot_general(qs_sc[p], k2, (((1,), (1,)), ((), ())), preferred_element_type=jnp.float32)
            chunks = []
            for c in range(n_chunks):
                sc = s[:, c * LANES:(c + 1) * LANES]
                if diagonal:
                    sc = jnp.where(col + c * LANES <= row, sc, -jnp.inf)
                chunks.append(sc)
            cq = cq_sc[p]
            m_old = m_sc[p]
            m_new = jnp.maximum(
                m_old, jnp.max(functools.reduce(jnp.maximum, chunks), axis=-1, keepdims=True) + cq)
            scale = jnp.exp(m_old - m_new)
            shift = m_new - cq
            probs = [jnp.exp(sc - shift) for sc in chunks]
            l_sc[p] = scale * l_sc[p] + functools.reduce(jnp.add, probs)
            pv = jnp.dot(jnp.concatenate([pc.astype(jnp.bfloat16) for pc in probs], axis=1), v2,
                         preferred_element_type=jnp.float32)
            acc_sc[p] = scale * acc_sc[p] + pv
            m_sc[p] = m_new

    def body(j, carry):
        step(j, diagonal=False)
        return carry

    lax.fori_loop(0, i, body, 0)
    step(i, diagonal=True)

    for p in range(N_PAIRS):
        o = acc_sc[p] / jnp.sum(l_sc[p], axis=-1, keepdims=True)
        o_ref[0, :, p * LANES:(p + 1) * LANES] = jnp.where(first_head, o[0:tq], o[tq:2 * tq]).astype(o_ref.dtype)


def _fox_attn(q, k, v, cum, kb):
    bsz, seq, _ = q.shape
    tq = ATTN_BLOCK
    stat = pltpu.VMEM((N_PAIRS, 2 * tq, LANES), jnp.float32)
    return pl.pallas_call(
        _fox_attn_kernel,
        grid=(bsz, seq // tq),
        in_specs=[
            pl.BlockSpec((1, tq, D_FOX), lambda b, i: (b, i, 0)),
            pl.BlockSpec((1, seq, D_FOX), lambda b, i: (b, 0, 0)),
            pl.BlockSpec((1, seq, D_FOX), lambda b, i: (b, 0, 0)),
            pl.BlockSpec((1, tq, LANES), lambda b, i: (b, i, 0)),
            pl.BlockSpec((1, seq, LANES), lambda b, i: (b, 0, 0)),
        ],
        out_specs=pl.BlockSpec((1, tq, D_FOX), lambda b, i: (b, i, 0)),
        out_shape=jax.ShapeDtypeStruct((bsz, seq, D_FOX), jnp.bfloat16),
        scratch_shapes=[pltpu.VMEM((N_PAIRS, 2 * tq, 2 * LANES), jnp.bfloat16), stat, stat, stat, stat],
        compiler_params=pltpu.CompilerParams(
            dimension_semantics=("arbitrary", "arbitrary"), vmem_limit_bytes=VMEM_LIMIT_BYTES),
        name="fox_attn",
    )(q, k, v, cum, kb)


def _mixer_out_kernel(x_ref, ya_ref, yb_ref, yc_ref, wa_ref, wb_ref, wc_ref, g_ref, b_ref, o_ref):
    mix = jnp.dot(ya_ref[...], wa_ref[...], preferred_element_type=jnp.float32)
    mix = mix + jnp.dot(yb_ref[...], wb_ref[...], preferred_element_type=jnp.float32)
    mix = mix + jnp.dot(yc_ref[...], wc_ref[...], preferred_element_type=jnp.float32)
    o_ref[...] = _layer_norm(ALPHA * x_ref[...] + mix, g_ref[...], b_ref[...])


def _mixer_out(x2d, ya, yb, yc, wa, wb, wc, g, b):
    rows = x2d.shape[0]
    row_block = lambda width: pl.BlockSpec((ROW_TILE, width), lambda i: (i, 0))
    return pl.pallas_call(
        _mixer_out_kernel,
        grid=(rows // ROW_TILE,),
        in_specs=[
            row_block(D_MODEL), row_block(D_CONV), row_block(D_FOX), row_block(D_SGU),
            _const_spec(wa.shape), _const_spec(wb.shape), _const_spec(wc.shape),
            _const_spec((1, D_MODEL)), _const_spec((1, D_MODEL)),
        ],
        out_specs=row_block(D_MODEL),
        out_shape=jax.ShapeDtypeStruct((rows, D_MODEL), jnp.float32),
        compiler_params=pltpu.CompilerParams(
            dimension_semantics=("arbitrary",), vmem_limit_bytes=VMEM_LIMIT_BYTES),
        name="mixer_out",
    )(x2d, ya, yb, yc, wa, wb, wc, g, b)


def kernel(x, ln1_g, ln1_b, ffn1_w_up, ffn1_w_down, mix_w_in, fox_b_f, conv_w, sgu_ln_g, sgu_ln_b, sgu_w_s,
           sgu_b_s, mix_w_out, ln2_g, ln2_b, ffn2_w_up, ffn2_w_down, ln3_g, ln3_b):
    bsz, seq, d = x.shape
    assert d == D_MODEL and seq % ROW_TILE == 0 and seq % ATTN_BLOCK == 0
    bf16 = jnp.bfloat16
    rows = bsz * seq
    row_vec = lambda p: p.reshape(1, -1)
    o_qkv = 3 * D_CONV
    o_f = o_qkv + 3 * D_FOX
    o_s = o_f + N_FOX_HEADS
    pad_f = LANES - N_FOX_HEADS

    h = x.reshape(rows, d)
    for l in range(DEPTH):
        h = _ffn_ln(h, ffn1_w_up[l].astype(bf16), ffn1_w_down[l].astype(bf16), row_vec(ln1_g[l]), row_vec(ln1_b[l]))

        w_in = mix_w_in[l]
        ya, yc, q, k, v, cum, kb = _mixer_in(
            h.reshape(bsz, seq, d),
            w_in[:, 0:o_qkv].astype(bf16),
            w_in[:, o_qkv:o_f].astype(bf16),
            w_in[:, o_s:o_s + 2 * D_SGU].astype(bf16),
            jnp.pad(w_in[:, o_f:o_s], ((0, 0), (0, pad_f))).astype(bf16),
            jnp.pad(fox_b_f[l], (0, pad_f)).reshape(1, LANES),
            conv_w[l], row_vec(sgu_ln_g[l]), row_vec(sgu_ln_b[l]),
            sgu_w_s[l],
            jnp.repeat(sgu_b_s[l].T, SGU_GROUP_DIM, axis=1))
        yb = _fox_attn(q, k, v, cum, kb)

        w_out = mix_w_out[l].astype(bf16)
        h = _mixer_out(
            h, ya.reshape(rows, D_CONV), yb.reshape(rows, D_FOX), yc.reshape(rows, D_SGU),
            w_out[0:D_CONV], w_out[D_CONV:D_CONV + D_FOX], w_out[D_CONV + D_FOX:],
            row_vec(ln2_g[l]), row_vec(ln2_b[l]))

        h = _ffn_ln(h, ffn2_w_up[l].astype(bf16), ffn2_w_down[l].astype(bf16), row_vec(ln3_g[l]), row_vec(ln3_b[l]))
    return h.reshape(bsz, seq, d)
```

```python
import jax
import jax.numpy as jnp
from jax import lax
from jax.experimental import pallas as pl
from jax.experimental.pallas import tpu as pltpu

D_MODEL = 1024
DEPTH = 2
D_CONV = 256
CONV_WIDTH = 3
D_FOX = 512
FOX_HEAD_DIM = 64
N_FOX_HEADS = 8
D_SGU = 256
N_SGU_GROUPS = 4
SGU_GROUP_DIM = 64
SGU_CHUNK = 128
D_FF = 2816
ALPHA = (2 * DEPTH) ** 0.25
LN_EPS = 1e-5

LANES = 128
CONV_HALO = 8
VMEM_LIMIT_BYTES = 56 * 1024 * 1024

ROW_TILE = 512
FF_CHUNK = 256
ATTN_BLOCK = 512


def _layer_norm(h, g, b):
    mu = jnp.mean(h, axis=-1, keepdims=True)
    hc = h - mu
    var = jnp.mean(hc * hc, axis=-1, keepdims=True)
    return hc * lax.rsqrt(var + LN_EPS) * g + b


def _const_spec(shape):
    return pl.BlockSpec(shape, lambda *_: (0,) * len(shape), pipeline_mode=pl.Buffered(1))


def _deferred_grid(n_tiles):
    return (n_tiles + 1,), (lambda i: jnp.minimum(i, n_tiles - 1)), (lambda i: jnp.maximum(i - 1, 0))


def _ffn_ln_kernel(x_ref, wup_ref, wdn_ref, g_ref, b_ref, o_ref, pre_sc):
    i = pl.program_id(0)
    last = pl.num_programs(0) - 1

    @pl.when(i == 0)
    def _():
        pre_sc[...] = jnp.zeros(pre_sc.shape, jnp.float32)

    @pl.when(i < last)
    def _():
        done = _layer_norm(pre_sc[...], g_ref[...], b_ref[...])
        o_ref[...] = done
        never = i < 0
        x = x_ref[...]
        xb = x.astype(jnp.bfloat16)
        acc = jnp.zeros(x.shape, jnp.float32)
        for c in range(D_FF // FF_CHUNK):
            lo = c * FF_CHUNK
            gate = jnp.dot(xb, wup_ref[:, lo:lo + FF_CHUNK], preferred_element_type=jnp.float32)
            up = jnp.dot(xb, wup_ref[:, D_FF + lo:D_FF + lo + FF_CHUNK], preferred_element_type=jnp.float32)
            act = (jax.nn.silu(gate) * up).astype(jnp.bfloat16)
            if 1 <= c <= D_MODEL // FF_CHUNK:
                act = jnp.where(never, done[:, (c - 1) * FF_CHUNK:c * FF_CHUNK].astype(jnp.bfloat16), act)
            acc = acc + jnp.dot(act, wdn_ref[lo:lo + FF_CHUNK, :], preferred_element_type=jnp.float32)
        pre_sc[...] = ALPHA * x + 0.5 * acc

    @pl.when(i == last)
    def _():
        o_ref[...] = _layer_norm(pre_sc[...], g_ref[...], b_ref[...])


def _ffn_ln(x2d, w_up, w_down, g, b):
    rows = x2d.shape[0]
    grid, in_tile, out_tile = _deferred_grid(rows // ROW_TILE)
    return pl.pallas_call(
        _ffn_ln_kernel,
        grid=grid,
        in_specs=[
            pl.BlockSpec((ROW_TILE, D_MODEL), lambda i: (in_tile(i), 0)),
            _const_spec((D_MODEL, 2 * D_FF)),
            _const_spec((D_FF, D_MODEL)),
            _const_spec((1, D_MODEL)),
            _const_spec((1, D_MODEL)),
        ],
        out_specs=pl.BlockSpec((ROW_TILE, D_MODEL), lambda i: (out_tile(i), 0)),
        out_shape=jax.ShapeDtypeStruct((rows, D_MODEL), jnp.float32),
        scratch_shapes=[pltpu.VMEM((ROW_TILE, D_MODEL), jnp.float32)],
        compiler_params=pltpu.CompilerParams(
            dimension_semantics=("arbitrary",), vmem_limit_bytes=VMEM_LIMIT_BYTES),
        name="ffn_ln",
    )(x2d, w_up, w_down, g, b)


def _log_sigmoid(x):
    return -(jnp.maximum(-x, 0.0) + jnp.log1p(jnp.exp(-jnp.abs(x))))


def _cumsum_rows(y):
    n = y.shape[0]
    row = lax.broadcasted_iota(jnp.int32, y.shape, 0)
    shift = 1
    while shift < n:
        y = y + jnp.where(row >= shift, pltpu.roll(y, shift, axis=0), 0.0)
        shift *= 2
    return y


def _mixer_in_kernel(x_ref, wall_ref, bf_ref, convw_ref, sg_ref, sb_ref, wsp_ref, bsp_ref,
                     ya_ref, yc_ref, q_ref, k_ref, v_ref, cum_ref, kb_ref,
                     zbuf, fcarry):
    j = pl.program_id(1)
    tm = x_ref.shape[1]

    @pl.when(j == 0)
    def _():
        zbuf[0:CONV_HALO, :] = jnp.zeros((CONV_HALO, D_CONV), jnp.float32)
        fcarry[...] = jnp.zeros(fcarry.shape, jnp.float32)

    xb = x_ref[0].astype(jnp.bfloat16)
    proj = jnp.dot(xb, wall_ref[...], preferred_element_type=jnp.float32)
    o_f = 2 * D_SGU
    o_c = o_f + LANES
    o_qkv = o_c + 3 * D_CONV

    ps = proj[:, 0:o_f]
    u = jax.nn.gelu(ps[:, 0:D_SGU])
    vn = _layer_norm(jax.nn.gelu(ps[:, D_SGU:2 * D_SGU]), sg_ref[...], sb_ref[...]).astype(jnp.bfloat16)
    r = lax.broadcasted_iota(jnp.int32, (SGU_CHUNK, SGU_CHUNK), 0)
    c = lax.broadcasted_iota(jnp.int32, (SGU_CHUNK, SGU_CHUNK), 1)
    causal = c <= r
    lane = lax.broadcasted_iota(jnp.int32, (SGU_CHUNK, LANES), 1)
    first_group = lane < SGU_GROUP_DIM
    bias = bsp_ref[...]
    for half in range(D_SGU // LANES):
        w_pair = jnp.concatenate(
            [jnp.where(causal, wsp_ref[2 * half], 0.0), jnp.where(causal, wsp_ref[2 * half + 1], 0.0)],
            axis=0).astype(jnp.bfloat16)
        for n in range(tm // SGU_CHUNK):
            rows = slice(n * SGU_CHUNK, (n + 1) * SGU_CHUNK)
            cols = slice(half * LANES, (half + 1) * LANES)
            both = jnp.dot(w_pair, vn[rows, cols], preferred_element_type=jnp.float32)
            mixed = jnp.where(first_group, both[0:SGU_CHUNK], both[SGU_CHUNK:2 * SGU_CHUNK]) + bias[:, cols]
            yc_ref[0, rows, cols] = (u[rows, cols] * mixed).astype(yc_ref.dtype)

    f_logit = proj[:, o_f:o_c] + bf_ref[...]
    cum = _cumsum_rows(_log_sigmoid(f_logit)) + fcarry[...]
    fcarry[...] = cum[tm - 1:tm, :]
    cum_ref[0] = cum
    lane = lax.broadcasted_iota(jnp.int32, cum.shape, 1)
    c0 = jnp.where(lane < N_FOX_HEADS, cum, 0.0)
    hi = c0.astype(jnp.bfloat16).astype(jnp.float32)
    r1 = c0 - hi
    mid = r1.astype(jnp.bfloat16).astype(jnp.float32)
    lo = (r1 - mid).astype(jnp.bfloat16).astype(jnp.float32)
    pieces = hi + pltpu.roll(mid, N_FOX_HEADS, axis=1) + pltpu.roll(lo, 2 * N_FOX_HEADS, axis=1)
    kb_ref[0] = (-pieces).astype(kb_ref.dtype)

    pc = proj[:, o_c:o_qkv]
    gate_b = pc[:, 0:D_CONV]
    z = pc[:, D_CONV:2 * D_CONV] * pc[:, 2 * D_CONV:3 * D_CONV]
    zbuf[CONV_HALO:CONV_HALO + tm, :] = z
    z1 = zbuf[CONV_HALO - 1:CONV_HALO - 1 + tm, :]
    z2 = zbuf[CONV_HALO - 2:CONV_HALO - 2 + tm, :]
    w = convw_ref[...]
    conv = w[0:1, :] * z2 + w[1:2, :] * z1 + w[2:3, :] * z
    ya_ref[0] = (gate_b * conv).astype(ya_ref.dtype)
    zbuf[0:CONV_HALO, :] = z[tm - CONV_HALO:tm, :]

    pqkv = proj[:, o_qkv:o_qkv + 3 * D_FOX]
    q_ref[0] = (pqkv[:, 0:D_FOX] * (FOX_HEAD_DIM ** -0.5)).astype(q_ref.dtype)
    k_ref[0] = pqkv[:, D_FOX:2 * D_FOX].astype(k_ref.dtype)
    v_ref[0] = pqkv[:, 2 * D_FOX:3 * D_FOX].astype(v_ref.dtype)


def _mixer_in(x3d, wall, bf, conv_w, sg, sb, w_sp, b_sp):
    bsz, seq, _ = x3d.shape
    tm = ROW_TILE
    row_block = lambda width: pl.BlockSpec((1, tm, width), lambda b, j: (b, j, 0))
    bf16 = jnp.bfloat16
    return pl.pallas_call(
        _mixer_in_kernel,
        grid=(bsz, seq // tm),
        in_specs=[
            row_block(D_MODEL),
            _const_spec(wall.shape), _const_spec(bf.shape), _const_spec(conv_w.shape),
            _const_spec(sg.shape), _const_spec(sb.shape), _const_spec(w_sp.shape), _const_spec(b_sp.shape),
        ],
        out_specs=[
            row_block(D_CONV), row_block(D_SGU), row_block(D_FOX), row_block(D_FOX), row_block(D_FOX),
            row_block(LANES), row_block(LANES),
        ],
        out_shape=[
            jax.ShapeDtypeStruct((bsz, seq, D_CONV), bf16),
            jax.ShapeDtypeStruct((bsz, seq, D_SGU), bf16),
            jax.ShapeDtypeStruct((bsz, seq, D_FOX), bf16),
            jax.ShapeDtypeStruct((bsz, seq, D_FOX), bf16),
            jax.ShapeDtypeStruct((bsz, seq, D_FOX), bf16),
            jax.ShapeDtypeStruct((bsz, seq, LANES), jnp.float32),
            jax.ShapeDtypeStruct((bsz, seq, LANES), bf16),
        ],
        scratch_shapes=[
            pltpu.VMEM((CONV_HALO + tm, D_CONV), jnp.float32),
            pltpu.VMEM((1, LANES), jnp.float32),
        ],
        compiler_params=pltpu.CompilerParams(
            dimension_semantics=("arbitrary", "arbitrary"), vmem_limit_bytes=VMEM_LIMIT_BYTES),
        name="mixer_in",
    )(x3d, wall, bf, conv_w, sg, sb, w_sp, b_sp)


N_PAIRS = D_FOX // LANES


def _fox_attn_kernel(q_ref, k_ref, v_ref, cum_ref, kb_ref, o_ref, qs_sc, cq_sc, m_sc, l_sc, acc_sc):
    i = pl.program_id(1)
    tq = q_ref.shape[1]
    tk = tq
    lane = lax.broadcasted_iota(jnp.int32, (tq, LANES), 1)
    first_head = lane < FOX_HEAD_DIM

    cum_t = cum_ref[0].T
    for p in range(N_PAIRS):
        q2 = q_ref[0, :, p * LANES:(p + 1) * LANES]
        zero = jnp.zeros_like(q2)
        qs_sc[p, 0:tq, 0:LANES] = jnp.where(first_head, q2, zero)
        qs_sc[p, tq:2 * tq, 0:LANES] = jnp.where(first_head, zero, q2)
        for hh in range(2):
            h = 2 * p + hh
            cq_sc[p, :, hh * tq:(hh + 1) * tq] = cum_t[h:h + 1, :]
            pick = (lane == h) | (lane == N_FOX_HEADS + h) | (lane == 2 * N_FOX_HEADS + h)
            qs_sc[p, hh * tq:(hh + 1) * tq, LANES:2 * LANES] = jnp.where(pick, 1.0, 0.0).astype(qs_sc.dtype)
    m_sc[...] = jnp.full(m_sc.shape, -jnp.inf, jnp.float32)
    l_sc[...] = jnp.zeros(l_sc.shape, jnp.float32)
    acc_sc[...] = jnp.zeros(acc_sc.shape, jnp.float32)

    key = lax.broadcasted_iota(jnp.int32, (tk, 2 * tq), 0)
    qry = lax.broadcasted_iota(jnp.int32, (tk, 2 * tq), 1)
    qry = jnp.where(qry >= tq, qry - tq, qry)

    def step(j, diagonal):
        start = pl.multiple_of(j * tk, tk)

        def scores(p):
            k2 = jnp.concatenate(
                [k_ref[0, pl.ds(start, tk), p * LANES:(p + 1) * LANES], kb_ref[0, pl.ds(start, tk), :]], axis=1)
            return lax.dot_general(k2, qs_sc[p], (((1,), (1,)), ((), ())),
                                   preferred_element_type=jnp.float32)

        s_next = scores(0)
        for p in range(N_PAIRS):
            s = s_next
            if p + 1 < N_PAIRS:
                s_next = scores(p + 1)
            v2 = v_ref[0, pl.ds(start, tk), p * LANES:(p + 1) * LANES]
            if diagonal:
                s = jnp.where(key <= qry, s, -jnp.inf)
            cq = cq_sc[p]
            m_old = m_sc[p]
            m_new = jnp.maximum(m_old, jnp.max(s, axis=0, keepdims=True) + cq)
            scale = jnp.exp(m_old - m_new)
            probs = jnp.exp(s - (m_new - cq))
            l_sc[p] = scale * l_sc[p] + jnp.sum(probs, axis=0, keepdims=True)
            pv = lax.dot_general(v2, probs.astype(jnp.bfloat16), (((0,), (0,)), ((), ())),
                                 preferred_element_type=jnp.float32)
            acc_sc[p] = scale * acc_sc[p] + pv
            m_sc[p] = m_new

    def body(j, carry):
        step(j, diagonal=False)
        return carry

    lax.fori_loop(0, i, body, 0)
    step(i, diagonal=True)

    dim = lax.broadcasted_iota(jnp.int32, (LANES, tq), 0)
    for p in range(N_PAIRS):
        o = acc_sc[p] / l_sc[p]
        o = jnp.where(dim < FOX_HEAD_DIM, o[:, 0:tq], o[:, tq:2 * tq])
        o_ref[0, :, p * LANES:(p + 1) * LANES] = o.T.astype(o_ref.dtype)


def _fox_attn(q, k, v, cum, kb):
    bsz, seq, _ = q.shape
    tq = ATTN_BLOCK
    stat = pltpu.VMEM((N_PAIRS, 1, 2 * tq), jnp.float32)
    return pl.pallas_call(
        _fox_attn_kernel,
        grid=(bsz, seq // tq),
        in_specs=[
            pl.BlockSpec((1, tq, D_FOX), lambda b, i: (b, i, 0)),
            pl.BlockSpec((1, seq, D_FOX), lambda b, i: (b, 0, 0)),
            pl.BlockSpec((1, seq, D_FOX), lambda b, i: (b, 0, 0)),
            pl.BlockSpec((1, tq, LANES), lambda b, i: (b, i, 0)),
            pl.BlockSpec((1, seq, LANES), lambda b, i: (b, 0, 0)),
        ],
        out_specs=pl.BlockSpec((1, tq, D_FOX), lambda b, i: (b, i, 0)),
        out_shape=jax.ShapeDtypeStruct((bsz, seq, D_FOX), jnp.bfloat16),
        scratch_shapes=[pltpu.VMEM((N_PAIRS, 2 * tq, 2 * LANES), jnp.bfloat16), stat, stat, stat,
                        pltpu.VMEM((N_PAIRS, LANES, 2 * tq), jnp.float32)],
        compiler_params=pltpu.CompilerParams(
            dimension_semantics=("arbitrary", "arbitrary"), vmem_limit_bytes=VMEM_LIMIT_BYTES),
        name="fox_attn",
    )(q, k, v, cum, kb)


def _mixer_out_kernel(x_ref, ya_ref, yb_ref, yc_ref, wa_ref, wb_ref, wc_ref, g_ref, b_ref, o_ref, pre_sc):
    i = pl.program_id(0)
    last = pl.num_programs(0) - 1

    @pl.when(i == 0)
    def _():
        pre_sc[...] = jnp.zeros(pre_sc.shape, jnp.float32)

    @pl.when(i < last)
    def _():
        o_ref[...] = _layer_norm(pre_sc[...], g_ref[...], b_ref[...])
        mix = jnp.dot(ya_ref[...], wa_ref[...], preferred_element_type=jnp.float32)
        mix = mix + jnp.dot(yb_ref[...], wb_ref[...], preferred_element_type=jnp.float32)
        mix = mix + jnp.dot(yc_ref[...], wc_ref[...], preferred_element_type=jnp.float32)
        pre_sc[...] = ALPHA * x_ref[...] + mix

    @pl.when(i == last)
    def _():
        o_ref[...] = _layer_norm(pre_sc[...], g_ref[...], b_ref[...])


def _mixer_out(x2d, ya, yb, yc, wa, wb, wc, g, b):
    rows = x2d.shape[0]
    grid, in_tile, out_tile = _deferred_grid(rows // ROW_TILE)
    row_block = lambda width: pl.BlockSpec((ROW_TILE, width), lambda i: (in_tile(i), 0))
    return pl.pallas_call(
        _mixer_out_kernel,
        grid=grid,
        in_specs=[
            row_block(D_MODEL), row_block(D_CONV), row_block(D_FOX), row_block(D_SGU),
            _const_spec(wa.shape), _const_spec(wb.shape), _const_spec(wc.shape),
            _const_spec((1, D_MODEL)), _const_spec((1, D_MODEL)),
        ],
        out_specs=pl.BlockSpec((ROW_TILE, D_MODEL), lambda i: (out_tile(i), 0)),
        out_shape=jax.ShapeDtypeStruct((rows, D_MODEL), jnp.float32),
        scratch_shapes=[pltpu.VMEM((ROW_TILE, D_MODEL), jnp.float32)],
        compiler_params=pltpu.CompilerParams(
            dimension_semantics=("arbitrary",), vmem_limit_bytes=VMEM_LIMIT_BYTES),
        name="mixer_out",
    )(x2d, ya, yb, yc, wa, wb, wc, g, b)


def kernel(x, ln1_g, ln1_b, ffn1_w_up, ffn1_w_down, mix_w_in, fox_b_f, conv_w, sgu_ln_g, sgu_ln_b, sgu_w_s,
           sgu_b_s, mix_w_out, ln2_g, ln2_b, ffn2_w_up, ffn2_w_down, ln3_g, ln3_b):
    bsz, seq, d = x.shape
    assert d == D_MODEL and seq % ROW_TILE == 0 and seq % ATTN_BLOCK == 0
    bf16 = jnp.bfloat16
    rows = bsz * seq
    row_vec = lambda p: p.reshape(1, -1)
    o_qkv = 3 * D_CONV
    o_f = o_qkv + 3 * D_FOX
    o_s = o_f + N_FOX_HEADS
    pad_f = LANES - N_FOX_HEADS

    h = x.reshape(rows, d)
    for l in range(DEPTH):
        h = _ffn_ln(h, ffn1_w_up[l].astype(bf16), ffn1_w_down[l].astype(bf16), row_vec(ln1_g[l]), row_vec(ln1_b[l]))

        w_in = mix_w_in[l]
        ya, yc, q, k, v, cum, kb = _mixer_in(
            h.reshape(bsz, seq, d),
            jnp.concatenate([w_in[:, o_s:o_s + 2 * D_SGU], jnp.pad(w_in[:, o_f:o_s], ((0, 0), (0, pad_f))),
                             w_in[:, 0:o_f]], axis=1).astype(bf16),
            jnp.pad(fox_b_f[l], (0, pad_f)).reshape(1, LANES),
            conv_w[l], row_vec(sgu_ln_g[l]), row_vec(sgu_ln_b[l]),
            sgu_w_s[l],
            jnp.repeat(sgu_b_s[l].T, SGU_GROUP_DIM, axis=1))
        yb = _fox_attn(q, k, v, cum, kb)

        w_out = mix_w_out[l].astype(bf16)
        h = _mixer_out(
            h, ya.reshape(rows, D_CONV), yb.reshape(rows, D_FOX), yc.reshape(rows, D_SGU),
            w_out[0:D_CONV], w_out[D_CONV:D_CONV + D_FOX], w_out[D_CONV + D_FOX:],
            row_vec(ln2_g[l]), row_vec(ln2_b[l]))

        h = _ffn_ln(h, ffn2_w_up[l].astype(bf16), ffn2_w_down[l].astype(bf16), row_vec(ln3_g[l]), row_vec(ln3_b[l]))
    return h.reshape(bsz, seq, d)
```

```python
import functools

import jax
import jax.numpy as jnp
from jax import lax
from jax.experimental import pallas as pl
from jax.experimental.pallas import tpu as pltpu

D_MODEL = 1024
DEPTH = 2
D_CONV = 256
CONV_WIDTH = 3
D_FOX = 512
FOX_HEAD_DIM = 64
N_FOX_HEADS = 8
D_SGU = 256
N_SGU_GROUPS = 4
SGU_GROUP_DIM = 64
SGU_CHUNK = 128
D_FF = 2816
ALPHA = (2 * DEPTH) ** 0.25
LN_EPS = 1e-5
LOG2E = 1.4426950408889634

LANES = 128
CONV_HALO = 8
VMEM_LIMIT_BYTES = 56 * 1024 * 1024

ROW_TILE = 512
FF_CHUNK = 256
ATTN_BLOCK = 512


def _layer_norm(h, g, b):
    mu = jnp.mean(h, axis=-1, keepdims=True)
    hc = h - mu
    var = jnp.mean(hc * hc, axis=-1, keepdims=True)
    return hc * lax.rsqrt(var + LN_EPS) * g + b


def _layer_spec(stacked, layer):
    zeros = (0,) * (stacked.ndim - 1)
    return pl.BlockSpec((None,) + stacked.shape[1:], lambda *_: (layer,) + zeros, pipeline_mode=pl.Buffered(1))


def _ffn_ln_kernel(*refs, mixed):
    if mixed:
        (x_ref, ya_ref, yb_ref, yc_ref, wo_ref, g2_ref, b2_ref,
         wup_ref, wdn_ref, g_ref, b_ref, o_ref, pre_sc, act_sc, h_sc) = refs
    else:
        x_ref, wup_ref, wdn_ref, g_ref, b_ref, o_ref, pre_sc, act_sc = refs
    lag = 1 if mixed else 0
    s = pl.program_id(0)
    last = pl.num_programs(0) - 1

    def mixer_out():
        y = jnp.concatenate([ya_ref[...], yb_ref[...], yc_ref[...]], axis=1)
        mix = jnp.dot(y, wo_ref[...], preferred_element_type=jnp.float32)
        return _layer_norm(ALPHA * x_ref[...] + mix, g2_ref[...], b2_ref[...])

    @pl.when(s == 0)
    def _():
        pre_sc[...] = jnp.zeros(pre_sc.shape, jnp.float32)
        if mixed:
            h_sc[...] = mixer_out()

    @pl.when((s >= lag) & (s < last))
    def _():
        done = _layer_norm(pre_sc[...], g_ref[...], b_ref[...])
        o_ref[...] = done
        h = h_sc[...] if mixed else x_ref[...]
        h_next = mixer_out() if mixed else None
        hb = h.astype(jnp.bfloat16)
        never = s < 0
        n_anchor = D_MODEL // FF_CHUNK
        for c in range(D_FF // FF_CHUNK):
            lo = c * FF_CHUNK
            gate = jnp.dot(hb, wup_ref[:, lo:lo + FF_CHUNK], preferred_element_type=jnp.float32)
            up = jnp.dot(hb, wup_ref[:, D_FF + lo:D_FF + lo + FF_CHUNK], preferred_element_type=jnp.float32)
            act = (jax.nn.silu(gate) * up).astype(jnp.bfloat16)
            if 1 <= c <= n_anchor:
                cols = slice((c - 1) * FF_CHUNK, c * FF_CHUNK)
                act = jnp.where(never, done[:, cols].astype(jnp.bfloat16), act)
            elif mixed and n_anchor + 2 <= c < 2 * n_anchor + 2:
                cols = slice((c - n_anchor - 2) * FF_CHUNK, (c - n_anchor - 1) * FF_CHUNK)
                act = jnp.where(never, h_next[:, cols].astype(jnp.bfloat16), act)
            act_sc[:, lo:lo + FF_CHUNK] = act
        acc = jnp.dot(act_sc[...], wdn_ref[...], preferred_element_type=jnp.float32)
        pre_sc[...] = ALPHA * h + 0.5 * acc
        if mixed:
            h_sc[...] = h_next

    @pl.when(s == last)
    def _():
        o_ref[...] = _layer_norm(pre_sc[...], g_ref[...], b_ref[...])


def _ffn_ln(x2d, layer, w_up, w_down, g, b, mixer=None):
    rows = x2d.shape[0]
    n_tiles = rows // ROW_TILE
    mixed = mixer is not None
    lag = 1 if mixed else 0
    in_tile = lambda s: jnp.minimum(s, n_tiles - 1)
    out_tile = lambda s: jnp.clip(s - lag - 1, 0, n_tiles - 1)
    row_block = lambda width: pl.BlockSpec((ROW_TILE, width), lambda s: (in_tile(s), 0))
    ffn_params = (w_up, w_down, g, b)
    in_specs = [row_block(D_MODEL)]
    operands = [x2d]
    scratch = [pltpu.VMEM((ROW_TILE, D_MODEL), jnp.float32), pltpu.VMEM((ROW_TILE, D_FF), jnp.bfloat16)]
    if mixed:
        ya, yb, yc, w_out, g2, b2 = mixer
        in_specs += [row_block(D_CONV), row_block(D_FOX), row_block(D_SGU)]
        in_specs += [_layer_spec(p, layer) for p in (w_out, g2, b2)]
        operands += [ya, yb, yc, w_out, g2, b2]
        scratch.append(pltpu.VMEM((ROW_TILE, D_MODEL), jnp.float32))
    in_specs += [_layer_spec(p, layer) for p in ffn_params]
    operands += list(ffn_params)
    return pl.pallas_call(
        functools.partial(_ffn_ln_kernel, mixed=mixed),
        grid=(n_tiles + 1 + lag,),
        in_specs=in_specs,
        out_specs=pl.BlockSpec((ROW_TILE, D_MODEL), lambda s: (out_tile(s), 0)),
        out_shape=jax.ShapeDtypeStruct((rows, D_MODEL), jnp.float32),
        scratch_shapes=scratch,
        compiler_params=pltpu.CompilerParams(
            dimension_semantics=("arbitrary",), vmem_limit_bytes=VMEM_LIMIT_BYTES),
        name="mix_ffn_ln" if mixed else "ffn_ln",
    )(*operands)


def _log_sigmoid(x):
    return -(jnp.maximum(-x, 0.0) + jnp.log1p(jnp.exp(-jnp.abs(x))))


def _cumsum_rows(y):
    n = y.shape[0]
    row = lax.broadcasted_iota(jnp.int32, y.shape, 0)
    shift = 1
    while shift < n:
        y = y + jnp.where(row >= shift, pltpu.roll(y, shift, axis=0), 0.0)
        shift *= 2
    return y


def _mixer_in_kernel(x_ref, wall_ref, bf_ref, convw_ref, sg_ref, sb_ref, wsp_ref, bsp_ref,
                     ya_ref, yc_ref, q_ref, k_ref, v_ref, cum_ref, kb_ref,
                     zbuf, fcarry):
    j = pl.program_id(1)
    tm = x_ref.shape[1]

    @pl.when(j == 0)
    def _():
        zbuf[0:CONV_HALO, :] = jnp.zeros((CONV_HALO, D_CONV), jnp.float32)
        fcarry[...] = jnp.zeros(fcarry.shape, jnp.float32)

    xb = x_ref[0].astype(jnp.bfloat16)
    proj = jnp.dot(xb, wall_ref[...], preferred_element_type=jnp.float32)
    o_f = 2 * D_SGU
    o_c = o_f + LANES
    o_qkv = o_c + 3 * D_CONV

    ps = proj[:, 0:o_f]
    u = jax.nn.gelu(ps[:, 0:D_SGU])
    vn = _layer_norm(jax.nn.gelu(ps[:, D_SGU:2 * D_SGU]), sg_ref[...], sb_ref[...]).astype(jnp.bfloat16)
    r = lax.broadcasted_iota(jnp.int32, (SGU_CHUNK, SGU_CHUNK), 0)
    c = lax.broadcasted_iota(jnp.int32, (SGU_CHUNK, SGU_CHUNK), 1)
    causal = c <= r
    lane = lax.broadcasted_iota(jnp.int32, (SGU_CHUNK, LANES), 1)
    first_group = lane < SGU_GROUP_DIM
    bias = bsp_ref[...]
    for half in range(D_SGU // LANES):
        w_pair = jnp.concatenate(
            [jnp.where(causal, wsp_ref[2 * half], 0.0), jnp.where(causal, wsp_ref[2 * half + 1], 0.0)],
            axis=0).astype(jnp.bfloat16)
        for n in range(tm // SGU_CHUNK):
            rows = slice(n * SGU_CHUNK, (n + 1) * SGU_CHUNK)
            cols = slice(half * LANES, (half + 1) * LANES)
            both = jnp.dot(w_pair, vn[rows, cols], preferred_element_type=jnp.float32)
            gated = jnp.where(first_group, both[0:SGU_CHUNK], both[SGU_CHUNK:2 * SGU_CHUNK]) + bias[:, cols]
            yc_ref[0, rows, cols] = (u[rows, cols] * gated).astype(yc_ref.dtype)

    f_logit = proj[:, o_f:o_c] + bf_ref[...]
    cum = _cumsum_rows(_log_sigmoid(f_logit)) + fcarry[...]
    fcarry[...] = cum[tm - 1:tm, :]
    cum = cum * LOG2E
    cum_ref[0] = cum
    lane = lax.broadcasted_iota(jnp.int32, cum.shape, 1)
    c0 = jnp.where(lane < N_FOX_HEADS, cum, 0.0)
    hi = c0.astype(jnp.bfloat16).astype(jnp.float32)
    r1 = c0 - hi
    mid = r1.astype(jnp.bfloat16).astype(jnp.float32)
    lo = (r1 - mid).astype(jnp.bfloat16).astype(jnp.float32)
    pieces = hi + pltpu.roll(mid, N_FOX_HEADS, axis=1) + pltpu.roll(lo, 2 * N_FOX_HEADS, axis=1)
    kb_ref[0] = (-pieces).astype(kb_ref.dtype)

    pc = proj[:, o_c:o_qkv]
    gate_b = pc[:, 0:D_CONV]
    z = pc[:, D_CONV:2 * D_CONV] * pc[:, 2 * D_CONV:3 * D_CONV]
    zbuf[CONV_HALO:CONV_HALO + tm, :] = z
    z1 = zbuf[CONV_HALO - 1:CONV_HALO - 1 + tm, :]
    z2 = zbuf[CONV_HALO - 2:CONV_HALO - 2 + tm, :]
    w = convw_ref[...]
    conv = w[0:1, :] * z2 + w[1:2, :] * z1 + w[2:3, :] * z
    ya_ref[0] = (gate_b * conv).astype(ya_ref.dtype)
    zbuf[0:CONV_HALO, :] = z[tm - CONV_HALO:tm, :]

    pqkv = proj[:, o_qkv:o_qkv + 3 * D_FOX]
    q_ref[0] = (pqkv[:, 0:D_FOX] * (FOX_HEAD_DIM ** -0.5 * LOG2E)).astype(q_ref.dtype)
    k_ref[0] = pqkv[:, D_FOX:2 * D_FOX].astype(k_ref.dtype)
    v_ref[0] = pqkv[:, 2 * D_FOX:3 * D_FOX].astype(v_ref.dtype)


def _mixer_in(x3d, layer, wall, bf, conv_w, sg, sb, w_sp, b_sp):
    bsz, seq, _ = x3d.shape
    tm = ROW_TILE
    row_block = lambda width: pl.BlockSpec((1, tm, width), lambda b, j: (b, j, 0))
    bf16 = jnp.bfloat16
    params = (wall, bf, conv_w, sg, sb, w_sp, b_sp)
    return pl.pallas_call(
        _mixer_in_kernel,
        grid=(bsz, seq // tm),
        in_specs=[row_block(D_MODEL)] + [_layer_spec(p, layer) for p in params],
        out_specs=[
            row_block(D_CONV), row_block(D_SGU), row_block(D_FOX), row_block(D_FOX), row_block(D_FOX),
            row_block(LANES), row_block(LANES),
        ],
        out_shape=[
            jax.ShapeDtypeStruct((bsz, seq, D_CONV), bf16),
            jax.ShapeDtypeStruct((bsz, seq, D_SGU), bf16),
            jax.ShapeDtypeStruct((bsz, seq, D_FOX), bf16),
            jax.ShapeDtypeStruct((bsz, seq, D_FOX), bf16),
            jax.ShapeDtypeStruct((bsz, seq, D_FOX), bf16),
            jax.ShapeDtypeStruct((bsz, seq, LANES), jnp.float32),
            jax.ShapeDtypeStruct((bsz, seq, LANES), bf16),
        ],
        scratch_shapes=[
            pltpu.VMEM((CONV_HALO + tm, D_CONV), jnp.float32),
            pltpu.VMEM((1, LANES), jnp.float32),
        ],
        compiler_params=pltpu.CompilerParams(
            dimension_semantics=("arbitrary", "arbitrary"), vmem_limit_bytes=VMEM_LIMIT_BYTES),
        name="mixer_in",
    )(x3d, *params)


N_PAIRS = D_FOX // LANES


def _fox_attn_kernel(q_ref, k_ref, v_ref, cum_ref, kb_ref, o_ref, qs_sc, cq_sc, m_sc, l_sc, acc_sc):
    i = pl.program_id(1)
    tq = q_ref.shape[1]
    tk = tq
    lane = lax.broadcasted_iota(jnp.int32, (tq, LANES), 1)
    first_head = lane < FOX_HEAD_DIM

    cum_t = cum_ref[0].T
    for p in range(N_PAIRS):
        q2 = q_ref[0, :, p * LANES:(p + 1) * LANES]
        zero = jnp.zeros_like(q2)
        qs_sc[p, 0:tq, 0:LANES] = jnp.where(first_head, q2, zero)
        qs_sc[p, tq:2 * tq, 0:LANES] = jnp.where(first_head, zero, q2)
        for hh in range(2):
            h = 2 * p + hh
            cq_sc[p, :, hh * tq:(hh + 1) * tq] = cum_t[h:h + 1, :]
            pick = (lane == h) | (lane == N_FOX_HEADS + h) | (lane == 2 * N_FOX_HEADS + h)
            qs_sc[p, hh * tq:(hh + 1) * tq, LANES:2 * LANES] = jnp.where(pick, 1.0, 0.0).astype(qs_sc.dtype)
    m_sc[...] = jnp.full(m_sc.shape, -jnp.inf, jnp.float32)
    l_sc[...] = jnp.zeros(l_sc.shape, jnp.float32)
    acc_sc[...] = jnp.zeros(acc_sc.shape, jnp.float32)

    key = lax.broadcasted_iota(jnp.int32, (tk, 2 * tq), 0)
    qry = lax.broadcasted_iota(jnp.int32, (tk, 2 * tq), 1)
    qry = jnp.where(qry >= tq, qry - tq, qry)

    def step(j, diagonal):
        start = pl.multiple_of(j * tk, tk)

        def scores(p):
            k2 = jnp.concatenate(
                [k_ref[0, pl.ds(start, tk), p * LANES:(p + 1) * LANES], kb_ref[0, pl.ds(start, tk), :]], axis=1)
            return lax.dot_general(k2, qs_sc[p], (((1,), (1,)), ((), ())),
                                   preferred_element_type=jnp.float32)

        s_next = scores(0)
        for p in range(N_PAIRS):
            s = s_next
            if p + 1 < N_PAIRS:
                s_next = scores(p + 1)
            v2 = v_ref[0, pl.ds(start, tk), p * LANES:(p + 1) * LANES]
            if diagonal:
                s = jnp.where(key <= qry, s, -jnp.inf)
            cq = cq_sc[p]
            m_old = m_sc[p]
            m_new = jnp.maximum(m_old, jnp.max(s, axis=0, keepdims=True) + cq)
            scale = jnp.exp2(m_old - m_new)
            probs = jnp.exp2(s - (m_new - cq))
            l_sc[p] = scale * l_sc[p] + jnp.sum(probs, axis=0, keepdims=True)
            pv = lax.dot_general(v2, probs.astype(jnp.bfloat16), (((0,), (0,)), ((), ())),
                                 preferred_element_type=jnp.float32)
            acc_sc[p] = scale * acc_sc[p] + pv
            m_sc[p] = m_new

    def body(j, carry):
        step(j, diagonal=False)
        return carry

    lax.fori_loop(0, i, body, 0)
    step(i, diagonal=True)

    dim = lax.broadcasted_iota(jnp.int32, (LANES, tq), 0)
    for p in range(N_PAIRS):
        o = acc_sc[p] / l_sc[p]
        o = jnp.where(dim < FOX_HEAD_DIM, o[:, 0:tq], o[:, tq:2 * tq])
        o_ref[0, :, p * LANES:(p + 1) * LANES] = o.T.astype(o_ref.dtype)


def _fox_attn(q, k, v, cum, kb):
    bsz, seq, _ = q.shape
    tq = ATTN_BLOCK
    stat = pltpu.VMEM((N_PAIRS, 1, 2 * tq), jnp.float32)
    return pl.pallas_call(
        _fox_attn_kernel,
        grid=(bsz, seq // tq),
        in_specs=[
            pl.BlockSpec((1, tq, D_FOX), lambda b, i: (b, i, 0)),
            pl.BlockSpec((1, seq, D_FOX), lambda b, i: (b, 0, 0)),
            pl.BlockSpec((1, seq, D_FOX), lambda b, i: (b, 0, 0)),
            pl.BlockSpec((1, tq, LANES), lambda b, i: (b, i, 0)),
            pl.BlockSpec((1, seq, LANES), lambda b, i: (b, 0, 0)),
        ],
        out_specs=pl.BlockSpec((1, tq, D_FOX), lambda b, i: (b, i, 0)),
        out_shape=jax.ShapeDtypeStruct((bsz, seq, D_FOX), jnp.bfloat16),
        scratch_shapes=[pltpu.VMEM((N_PAIRS, 2 * tq, 2 * LANES), jnp.bfloat16), stat, stat, stat,
                        pltpu.VMEM((N_PAIRS, LANES, 2 * tq), jnp.float32)],
        compiler_params=pltpu.CompilerParams(
            dimension_semantics=("arbitrary", "arbitrary"), vmem_limit_bytes=VMEM_LIMIT_BYTES),
        name="fox_attn",
    )(q, k, v, cum, kb)


def kernel(x, ln1_g, ln1_b, ffn1_w_up, ffn1_w_down, mix_w_in, fox_b_f, conv_w, sgu_ln_g, sgu_ln_b, sgu_w_s,
           sgu_b_s, mix_w_out, ln2_g, ln2_b, ffn2_w_up, ffn2_w_down, ln3_g, ln3_b):
    bsz, seq, d = x.shape
    assert d == D_MODEL and seq % ROW_TILE == 0 and seq % ATTN_BLOCK == 0
    bf16 = jnp.bfloat16
    rows = bsz * seq
    row_vecs = lambda p: p.reshape(p.shape[0], 1, -1)
    o_qkv = 3 * D_CONV
    o_f = o_qkv + 3 * D_FOX
    o_s = o_f + N_FOX_HEADS
    pad_f = LANES - N_FOX_HEADS

    w_up1, w_dn1 = ffn1_w_up.astype(bf16), ffn1_w_down.astype(bf16)
    w_up2, w_dn2 = ffn2_w_up.astype(bf16), ffn2_w_down.astype(bf16)
    wall = jnp.concatenate(
        [mix_w_in[:, :, o_s:o_s + 2 * D_SGU], jnp.pad(mix_w_in[:, :, o_f:o_s], ((0, 0), (0, 0), (0, pad_f))),
         mix_w_in[:, :, 0:o_f]], axis=2).astype(bf16)
    b_f = jnp.pad(fox_b_f, ((0, 0), (0, pad_f))).reshape(DEPTH, 1, LANES)
    b_sp = jnp.repeat(jnp.swapaxes(sgu_b_s, 1, 2), SGU_GROUP_DIM, axis=2)
    w_out = mix_w_out.astype(bf16)
    g1, b1, g2, b2, g3, b3 = (row_vecs(p) for p in (ln1_g, ln1_b, ln2_g, ln2_b, ln3_g, ln3_b))
    sg, sb = row_vecs(sgu_ln_g), row_vecs(sgu_ln_b)

    h = x.reshape(rows, d)
    for l in range(DEPTH):
        h = _ffn_ln(h, l, w_up1, w_dn1, g1, b1)
        ya, yc, q, k, v, cum, kb = _mixer_in(h.reshape(bsz, seq, d), l, wall, b_f, conv_w, sg, sb, sgu_w_s, b_sp)
        yb = _fox_attn(q, k, v, cum, kb)
        mixer = (ya.reshape(rows, D_CONV), yb.reshape(rows, D_FOX), yc.reshape(rows, D_SGU), w_out, g2, b2)
        h = _ffn_ln(h, l, w_up2, w_dn2, g3, b3, mixer=mixer)
    return h.reshape(bsz, seq, d)
```

```python
import functools

import jax
import jax.numpy as jnp
from jax import lax
from jax.experimental import pallas as pl
from jax.experimental.pallas import tpu as pltpu

D_MODEL = 1024
DEPTH = 2
D_CONV = 256
CONV_WIDTH = 3
D_FOX = 512
FOX_HEAD_DIM = 64
N_FOX_HEADS = 8
D_SGU = 256
N_SGU_GROUPS = 4
SGU_GROUP_DIM = 64
SGU_CHUNK = 128
D_FF = 2816
ALPHA = (2 * DEPTH) ** 0.25
LN_EPS = 1e-5
LOG2E = 1.4426950408889634

LANES = 128
CONV_HALO = 8
VMEM_LIMIT_BYTES = 56 * 1024 * 1024

ROW_TILE = 512
FF_CHUNK = 256
ATTN_BLOCK = 512


def _layer_norm(h, g, b):
    mu = jnp.mean(h, axis=-1, keepdims=True)
    hc = h - mu
    var = jnp.mean(hc * hc, axis=-1, keepdims=True)
    return hc * lax.rsqrt(var + LN_EPS) * g + b


def _layer_spec(stacked, layer):
    zeros = (0,) * (stacked.ndim - 1)
    return pl.BlockSpec((None,) + stacked.shape[1:], lambda *_: (layer,) + zeros, pipeline_mode=pl.Buffered(1))


def _ffn_ln_kernel(*refs, mixed):
    if mixed:
        (x_ref, ya_ref, yb_ref, yc_ref, wo_ref, g2_ref, b2_ref,
         wup_ref, wdn_ref, g_ref, b_ref, o_ref, pre_sc, act_sc, h_sc) = refs
    else:
        x_ref, wup_ref, wdn_ref, g_ref, b_ref, o_ref, pre_sc, act_sc = refs
    lag = 1 if mixed else 0
    s = pl.program_id(0)
    last = pl.num_programs(0) - 1

    def mixer_out():
        y = jnp.concatenate([ya_ref[...], yb_ref[...], yc_ref[...]], axis=1)
        mix = jnp.dot(y, wo_ref[...], preferred_element_type=jnp.float32)
        return _layer_norm(ALPHA * x_ref[...] + mix, g2_ref[...], b2_ref[...])

    @pl.when(s == 0)
    def _():
        pre_sc[...] = jnp.zeros(pre_sc.shape, jnp.float32)
        if mixed:
            h_sc[...] = mixer_out()

    @pl.when((s >= lag) & (s < last))
    def _():
        done = _layer_norm(pre_sc[...], g_ref[...], b_ref[...])
        o_ref[...] = done
        h = h_sc[...] if mixed else x_ref[...]
        h_next = mixer_out() if mixed else None
        hb = h.astype(jnp.bfloat16)
        never = s < 0
        n_anchor = D_MODEL // FF_CHUNK
        for c in range(D_FF // FF_CHUNK):
            lo = c * FF_CHUNK
            gate = jnp.dot(hb, wup_ref[:, lo:lo + FF_CHUNK], preferred_element_type=jnp.float32)
            up = jnp.dot(hb, wup_ref[:, D_FF + lo:D_FF + lo + FF_CHUNK], preferred_element_type=jnp.float32)
            act = (jax.nn.silu(gate) * up).astype(jnp.bfloat16)
            if 1 <= c <= n_anchor:
                cols = slice((c - 1) * FF_CHUNK, c * FF_CHUNK)
                act = jnp.where(never, done[:, cols].astype(jnp.bfloat16), act)
            elif mixed and n_anchor + 2 <= c < 2 * n_anchor + 2:
                cols = slice((c - n_anchor - 2) * FF_CHUNK, (c - n_anchor - 1) * FF_CHUNK)
                act = jnp.where(never, h_next[:, cols].astype(jnp.bfloat16), act)
            act_sc[:, lo:lo + FF_CHUNK] = act
        acc = jnp.dot(act_sc[...], wdn_ref[...], preferred_element_type=jnp.float32)
        pre_sc[...] = ALPHA * h + 0.5 * acc
        if mixed:
            h_sc[...] = h_next

    @pl.when(s == last)
    def _():
        o_ref[...] = _layer_norm(pre_sc[...], g_ref[...], b_ref[...])


def _ffn_ln(x2d, layer, w_up, w_down, g, b, mixer=None):
    rows = x2d.shape[0]
    n_tiles = rows // ROW_TILE
    mixed = mixer is not None
    lag = 1 if mixed else 0
    in_tile = lambda s: jnp.minimum(s, n_tiles - 1)
    out_tile = lambda s: jnp.clip(s - lag - 1, 0, n_tiles - 1)
    row_block = lambda width: pl.BlockSpec((ROW_TILE, width), lambda s: (in_tile(s), 0))
    ffn_params = (w_up, w_down, g, b)
    in_specs = [row_block(D_MODEL)]
    operands = [x2d]
    scratch = [pltpu.VMEM((ROW_TILE, D_MODEL), jnp.float32), pltpu.VMEM((ROW_TILE, D_FF), jnp.bfloat16)]
    if mixed:
        ya, yb, yc, w_out, g2, b2 = mixer
        in_specs += [row_block(D_CONV), row_block(D_FOX), row_block(D_SGU)]
        in_specs += [_layer_spec(p, layer) for p in (w_out, g2, b2)]
        operands += [ya, yb, yc, w_out, g2, b2]
        scratch.append(pltpu.VMEM((ROW_TILE, D_MODEL), jnp.float32))
    in_specs += [_layer_spec(p, layer) for p in ffn_params]
    operands += list(ffn_params)
    return pl.pallas_call(
        functools.partial(_ffn_ln_kernel, mixed=mixed),
        grid=(n_tiles + 1 + lag,),
        in_specs=in_specs,
        out_specs=pl.BlockSpec((ROW_TILE, D_MODEL), lambda s: (out_tile(s), 0)),
        out_shape=jax.ShapeDtypeStruct((rows, D_MODEL), jnp.float32),
        scratch_shapes=scratch,
        compiler_params=pltpu.CompilerParams(
            dimension_semantics=("arbitrary",), vmem_limit_bytes=VMEM_LIMIT_BYTES),
        name="mix_ffn_ln" if mixed else "ffn_ln",
    )(*operands)


def _log_sigmoid(x):
    return -(jnp.maximum(-x, 0.0) + jnp.log1p(jnp.exp(-jnp.abs(x))))


def _cumsum_rows(y):
    n = y.shape[0]
    row = lax.broadcasted_iota(jnp.int32, y.shape, 0)
    shift = 1
    while shift < n:
        y = y + jnp.where(row >= shift, pltpu.roll(y, shift, axis=0), 0.0)
        shift *= 2
    return y


def _mixer_in_kernel(x_ref, wall_ref, bf_ref, convw_ref, sg_ref, sb_ref, wsp_ref, bsp_ref,
                     ya_ref, yc_ref, q_ref, k_ref, v_ref, cum_ref, kb_ref,
                     zbuf, fcarry):
    j = pl.program_id(1)
    tm = x_ref.shape[1]

    @pl.when(j == 0)
    def _():
        zbuf[0:CONV_HALO, :] = jnp.zeros((CONV_HALO, D_CONV), jnp.float32)
        fcarry[...] = jnp.zeros(fcarry.shape, jnp.float32)

    xb = x_ref[0].astype(jnp.bfloat16)
    proj = jnp.dot(xb, wall_ref[...], preferred_element_type=jnp.float32)
    o_f = 2 * D_SGU
    o_c = o_f + LANES
    o_qkv = o_c + 3 * D_CONV

    ps = proj[:, 0:o_f]
    u = jax.nn.gelu(ps[:, 0:D_SGU])
    vn = _layer_norm(jax.nn.gelu(ps[:, D_SGU:2 * D_SGU]), sg_ref[...], sb_ref[...]).astype(jnp.bfloat16)
    r = lax.broadcasted_iota(jnp.int32, (SGU_CHUNK, SGU_CHUNK), 0)
    c = lax.broadcasted_iota(jnp.int32, (SGU_CHUNK, SGU_CHUNK), 1)
    causal = c <= r
    lane = lax.broadcasted_iota(jnp.int32, (SGU_CHUNK, LANES), 1)
    first_group = lane < SGU_GROUP_DIM
    bias = bsp_ref[...]
    for half in range(D_SGU // LANES):
        w_pair = jnp.concatenate(
            [jnp.where(causal, wsp_ref[2 * half], 0.0), jnp.where(causal, wsp_ref[2 * half + 1], 0.0)],
            axis=0).astype(jnp.bfloat16)
        for n in range(tm // SGU_CHUNK):
            rows = slice(n * SGU_CHUNK, (n + 1) * SGU_CHUNK)
            cols = slice(half * LANES, (half + 1) * LANES)
            both = jnp.dot(w_pair, vn[rows, cols], preferred_element_type=jnp.float32)
            gated = jnp.where(first_group, both[0:SGU_CHUNK], both[SGU_CHUNK:2 * SGU_CHUNK]) + bias[:, cols]
            yc_ref[0, rows, cols] = (u[rows, cols] * gated).astype(yc_ref.dtype)

    f_logit = proj[:, o_f:o_c] + bf_ref[...]
    cum = _cumsum_rows(_log_sigmoid(f_logit)) + fcarry[...]
    fcarry[...] = cum[tm - 1:tm, :]
    cum = cum * LOG2E
    cum_ref[0] = cum
    lane = lax.broadcasted_iota(jnp.int32, cum.shape, 1)
    c0 = jnp.where(lane < N_FOX_HEADS, cum, 0.0)
    hi = c0.astype(jnp.bfloat16).astype(jnp.float32)
    r1 = c0 - hi
    mid = r1.astype(jnp.bfloat16).astype(jnp.float32)
    lo = (r1 - mid).astype(jnp.bfloat16).astype(jnp.float32)
    pieces = hi + pltpu.roll(mid, N_FOX_HEADS, axis=1) + pltpu.roll(lo, 2 * N_FOX_HEADS, axis=1)
    kb_ref[0] = (-pieces).astype(kb_ref.dtype)

    pc = proj[:, o_c:o_qkv]
    gate_b = pc[:, 0:D_CONV]
    z = pc[:, D_CONV:2 * D_CONV] * pc[:, 2 * D_CONV:3 * D_CONV]
    zbuf[CONV_HALO:CONV_HALO + tm, :] = z
    z1 = zbuf[CONV_HALO - 1:CONV_HALO - 1 + tm, :]
    z2 = zbuf[CONV_HALO - 2:CONV_HALO - 2 + tm, :]
    w = convw_ref[...]
    conv = w[0:1, :] * z2 + w[1:2, :] * z1 + w[2:3, :] * z
    ya_ref[0] = (gate_b * conv).astype(ya_ref.dtype)
    zbuf[0:CONV_HALO, :] = z[tm - CONV_HALO:tm, :]

    pqkv = proj[:, o_qkv:o_qkv + 3 * D_FOX]
    q_ref[0] = (pqkv[:, 0:D_FOX] * (FOX_HEAD_DIM ** -0.5 * LOG2E)).astype(q_ref.dtype)
    k_ref[0] = pqkv[:, D_FOX:2 * D_FOX].astype(k_ref.dtype)
    v_ref[0] = pqkv[:, 2 * D_FOX:3 * D_FOX].astype(v_ref.dtype)


def _mixer_in(x3d, layer, wall, bf, conv_w, sg, sb, w_sp, b_sp):
    bsz, seq, _ = x3d.shape
    tm = ROW_TILE
    row_block = lambda width: pl.BlockSpec((1, tm, width), lambda b, j: (b, j, 0))
    bf16 = jnp.bfloat16
    params = (wall, bf, conv_w, sg, sb, w_sp, b_sp)
    return pl.pallas_call(
        _mixer_in_kernel,
        grid=(bsz, seq // tm),
        in_specs=[row_block(D_MODEL)] + [_layer_spec(p, layer) for p in params],
        out_specs=[
            row_block(D_CONV), row_block(D_SGU), row_block(D_FOX), row_block(D_FOX), row_block(D_FOX),
            row_block(LANES), row_block(LANES),
        ],
        out_shape=[
            jax.ShapeDtypeStruct((bsz, seq, D_CONV), bf16),
            jax.ShapeDtypeStruct((bsz, seq, D_SGU), bf16),
            jax.ShapeDtypeStruct((bsz, seq, D_FOX), bf16),
            jax.ShapeDtypeStruct((bsz, seq, D_FOX), bf16),
            jax.ShapeDtypeStruct((bsz, seq, D_FOX), bf16),
            jax.ShapeDtypeStruct((bsz, seq, LANES), jnp.float32),
            jax.ShapeDtypeStruct((bsz, seq, LANES), bf16),
        ],
        scratch_shapes=[
            pltpu.VMEM((CONV_HALO + tm, D_CONV), jnp.float32),
            pltpu.VMEM((1, LANES), jnp.float32),
        ],
        compiler_params=pltpu.CompilerParams(
            dimension_semantics=("arbitrary", "arbitrary"), vmem_limit_bytes=VMEM_LIMIT_BYTES),
        name="mixer_in",
    )(x3d, *params)


N_PAIRS = D_FOX // LANES


def _fox_attn_kernel(q_ref, k_ref, v_ref, cum_ref, kb_ref, o_ref, qs_sc, cq_sc, m_sc, l_sc, acc_sc, s_sc):
    i = pl.program_id(1)
    tq = q_ref.shape[1]
    tk = tq
    lane = lax.broadcasted_iota(jnp.int32, (tq, LANES), 1)
    first_head = lane < FOX_HEAD_DIM

    cum_t = cum_ref[0].T
    for p in range(N_PAIRS):
        q2 = q_ref[0, :, p * LANES:(p + 1) * LANES]
        zero = jnp.zeros_like(q2)
        qs_sc[p, 0:tq, 0:LANES] = jnp.where(first_head, q2, zero)
        qs_sc[p, tq:2 * tq, 0:LANES] = jnp.where(first_head, zero, q2)
        for hh in range(2):
            h = 2 * p + hh
            cq_sc[p, :, hh * tq:(hh + 1) * tq] = cum_t[h:h + 1, :]
            pick = (lane == h) | (lane == N_FOX_HEADS + h) | (lane == 2 * N_FOX_HEADS + h)
            qs_sc[p, hh * tq:(hh + 1) * tq, LANES:2 * LANES] = jnp.where(pick, 1.0, 0.0).astype(qs_sc.dtype)
    m_sc[...] = jnp.full(m_sc.shape, -jnp.inf, jnp.float32)
    l_sc[...] = jnp.zeros(l_sc.shape, jnp.float32)
    acc_sc[...] = jnp.zeros(acc_sc.shape, jnp.float32)

    key = lax.broadcasted_iota(jnp.int32, (tk, 2 * tq), 0)
    qry = lax.broadcasted_iota(jnp.int32, (tk, 2 * tq), 1)
    qry = jnp.where(qry >= tq, qry - tq, qry)

    def step(j, diagonal):
        start = pl.multiple_of(j * tk, tk)

        def scores(p):
            k2 = jnp.concatenate(
                [k_ref[0, pl.ds(start, tk), p * LANES:(p + 1) * LANES], kb_ref[0, pl.ds(start, tk), :]], axis=1)
            s_sc[p % 2] = lax.dot_general(k2, qs_sc[p], (((1,), (1,)), ((), ())),
                                          preferred_element_type=jnp.float32)

        scores(0)
        for p in range(N_PAIRS):
            if p + 1 < N_PAIRS:
                scores(p + 1)
            s = s_sc[p % 2]
            v2 = v_ref[0, pl.ds(start, tk), p * LANES:(p + 1) * LANES]
            if diagonal:
                s = jnp.where(key <= qry, s, -jnp.inf)
            cq = cq_sc[p]
            m_old = m_sc[p]
            m_new = jnp.maximum(m_old, jnp.max(s, axis=0, keepdims=True) + cq)
            scale = jnp.exp2(m_old - m_new)
            probs = jnp.exp2(s - (m_new - cq))
            l_sc[p] = scale * l_sc[p] + jnp.sum(probs, axis=0, keepdims=True)
            pv = lax.dot_general(v2, probs.astype(jnp.bfloat16), (((0,), (0,)), ((), ())),
                                 preferred_element_type=jnp.float32)
            acc_sc[p] = scale * acc_sc[p] + pv
            m_sc[p] = m_new

    def body(j, carry):
        step(j, diagonal=False)
        return carry

    lax.fori_loop(0, i, body, 0)
    step(i, diagonal=True)

    dim = lax.broadcasted_iota(jnp.int32, (LANES, tq), 0)
    for p in range(N_PAIRS):
        o = acc_sc[p] / l_sc[p]
        o = jnp.where(dim < FOX_HEAD_DIM, o[:, 0:tq], o[:, tq:2 * tq])
        o_ref[0, :, p * LANES:(p + 1) * LANES] = o.T.astype(o_ref.dtype)


def _fox_attn(q, k, v, cum, kb):
    bsz, seq, _ = q.shape
    tq = ATTN_BLOCK
    stat = pltpu.VMEM((N_PAIRS, 1, 2 * tq), jnp.float32)
    return pl.pallas_call(
        _fox_attn_kernel,
        grid=(bsz, seq // tq),
        in_specs=[
            pl.BlockSpec((1, tq, D_FOX), lambda b, i: (b, i, 0)),
            pl.BlockSpec((1, seq, D_FOX), lambda b, i: (b, 0, 0)),
            pl.BlockSpec((1, seq, D_FOX), lambda b, i: (b, 0, 0)),
            pl.BlockSpec((1, tq, LANES), lambda b, i: (b, i, 0)),
            pl.BlockSpec((1, seq, LANES), lambda b, i: (b, 0, 0)),
        ],
        out_specs=pl.BlockSpec((1, tq, D_FOX), lambda b, i: (b, i, 0)),
        out_shape=jax.ShapeDtypeStruct((bsz, seq, D_FOX), jnp.bfloat16),
        scratch_shapes=[pltpu.VMEM((N_PAIRS, 2 * tq, 2 * LANES), jnp.bfloat16), stat, stat, stat,
                        pltpu.VMEM((N_PAIRS, LANES, 2 * tq), jnp.float32),
                        pltpu.VMEM((2, tq, 2 * tq), jnp.float32)],
        compiler_params=pltpu.CompilerParams(
            dimension_semantics=("arbitrary", "arbitrary"), vmem_limit_bytes=VMEM_LIMIT_BYTES),
        name="fox_attn",
    )(q, k, v, cum, kb)


def kernel(x, ln1_g, ln1_b, ffn1_w_up, ffn1_w_down, mix_w_in, fox_b_f, conv_w, sgu_ln_g, sgu_ln_b, sgu_w_s,
           sgu_b_s, mix_w_out, ln2_g, ln2_b, ffn2_w_up, ffn2_w_down, ln3_g, ln3_b):
    bsz, seq, d = x.shape
    assert d == D_MODEL and seq % ROW_TILE == 0 and seq % ATTN_BLOCK == 0
    bf16 = jnp.bfloat16
    rows = bsz * seq
    row_vecs = lambda p: p.reshape(p.shape[0], 1, -1)
    o_qkv = 3 * D_CONV
    o_f = o_qkv + 3 * D_FOX
    o_s = o_f + N_FOX_HEADS
    pad_f = LANES - N_FOX_HEADS

    w_up1, w_dn1 = ffn1_w_up.astype(bf16), ffn1_w_down.astype(bf16)
    w_up2, w_dn2 = ffn2_w_up.astype(bf16), ffn2_w_down.astype(bf16)
    wall = jnp.concatenate(
        [mix_w_in[:, :, o_s:o_s + 2 * D_SGU], jnp.pad(mix_w_in[:, :, o_f:o_s], ((0, 0), (0, 0), (0, pad_f))),
         mix_w_in[:, :, 0:o_f]], axis=2).astype(bf16)
    b_f = jnp.pad(fox_b_f, ((0, 0), (0, pad_f))).reshape(DEPTH, 1, LANES)
    b_sp = jnp.repeat(jnp.swapaxes(sgu_b_s, 1, 2), SGU_GROUP_DIM, axis=2)
    w_out = mix_w_out.astype(bf16)
    g1, b1, g2, b2, g3, b3 = (row_vecs(p) for p in (ln1_g, ln1_b, ln2_g, ln2_b, ln3_g, ln3_b))
    sg, sb = row_vecs(sgu_ln_g), row_vecs(sgu_ln_b)

    h = x.reshape(rows, d)
    for l in range(DEPTH):
        h = _ffn_ln(h, l, w_up1, w_dn1, g1, b1)
        ya, yc, q, k, v, cum, kb = _mixer_in(h.reshape(bsz, seq, d), l, wall, b_f, conv_w, sg, sb, sgu_w_s, b_sp)
        yb = _fox_attn(q, k, v, cum, kb)
        mixer = (ya.reshape(rows, D_CONV), yb.reshape(rows, D_FOX), yc.reshape(rows, D_SGU), w_out, g2, b2)
        h = _ffn_ln(h, l, w_up2, w_dn2, g3, b3, mixer=mixer)
    return h.reshape(bsz, seq, d)
```

```python
import functools

import jax
import jax.numpy as jnp
from jax import lax
from jax.experimental import pallas as pl
from jax.experimental.pallas import tpu as pltpu

D_MODEL = 1024
DEPTH = 2
D_CONV = 256
CONV_WIDTH = 3
D_FOX = 512
FOX_HEAD_DIM = 64
N_FOX_HEADS = 8
D_SGU = 256
N_SGU_GROUPS = 4
SGU_GROUP_DIM = 64
SGU_CHUNK = 128
D_FF = 2816
ALPHA = (2 * DEPTH) ** 0.25
LN_EPS = 1e-5
LOG2E = 1.4426950408889634

LANES = 128
CONV_HALO = 8
VMEM_LIMIT_BYTES = 56 * 1024 * 1024

ROW_TILE = 512
MIXER_TILE = 1024
FF_CHUNK = 256
ATTN_BLOCK = 512


def _layer_norm(h, g, b):
    mu = jnp.mean(h, axis=-1, keepdims=True)
    hc = h - mu
    var = jnp.mean(hc * hc, axis=-1, keepdims=True)
    return hc * lax.rsqrt(var + LN_EPS) * g + b


def _layer_spec(stacked, layer):
    zeros = (0,) * (stacked.ndim - 1)
    return pl.BlockSpec((None,) + stacked.shape[1:], lambda *_: (layer,) + zeros, pipeline_mode=pl.Buffered(1))


def _ffn_ln_kernel(*refs, mixed):
    if mixed:
        (x_ref, ya_ref, yb_ref, yc_ref, wo_ref, g2_ref, b2_ref,
         wup_ref, wdn_ref, g_ref, b_ref, o_ref, pre_sc, act_sc, h_sc) = refs
    else:
        x_ref, wup_ref, wdn_ref, g_ref, b_ref, o_ref, pre_sc, act_sc = refs
    lag = 1 if mixed else 0
    s = pl.program_id(0)
    last = pl.num_programs(0) - 1

    def mixer_out():
        y = jnp.concatenate([ya_ref[...], yb_ref[...], yc_ref[...]], axis=1)
        mix = jnp.dot(y, wo_ref[...], preferred_element_type=jnp.float32)
        return _layer_norm(ALPHA * x_ref[...] + mix, g2_ref[...], b2_ref[...])

    @pl.when(s == 0)
    def _():
        pre_sc[...] = jnp.zeros(pre_sc.shape, jnp.float32)
        if mixed:
            h_sc[...] = mixer_out()

    @pl.when((s >= lag) & (s < last))
    def _():
        done = _layer_norm(pre_sc[...], g_ref[...], b_ref[...])
        o_ref[...] = done
        h = h_sc[...] if mixed else x_ref[...]
        h_next = mixer_out() if mixed else None
        hb = h.astype(jnp.bfloat16)
        never = s < 0
        n_anchor = D_MODEL // FF_CHUNK
        for c in range(D_FF // FF_CHUNK):
            lo = c * FF_CHUNK
            gate = jnp.dot(hb, wup_ref[:, lo:lo + FF_CHUNK], preferred_element_type=jnp.float32)
            up = jnp.dot(hb, wup_ref[:, D_FF + lo:D_FF + lo + FF_CHUNK], preferred_element_type=jnp.float32)
            act = (jax.nn.silu(gate) * up).astype(jnp.bfloat16)
            if 1 <= c <= n_anchor:
                cols = slice((c - 1) * FF_CHUNK, c * FF_CHUNK)
                act = jnp.where(never, done[:, cols].astype(jnp.bfloat16), act)
            elif mixed and n_anchor + 2 <= c < 2 * n_anchor + 2:
                cols = slice((c - n_anchor - 2) * FF_CHUNK, (c - n_anchor - 1) * FF_CHUNK)
                act = jnp.where(never, h_next[:, cols].astype(jnp.bfloat16), act)
            act_sc[:, lo:lo + FF_CHUNK] = act
        acc = jnp.dot(act_sc[...], wdn_ref[...], preferred_element_type=jnp.float32)
        pre_sc[...] = ALPHA * h + 0.5 * acc
        if mixed:
            h_sc[...] = h_next

    @pl.when(s == last)
    def _():
        o_ref[...] = _layer_norm(pre_sc[...], g_ref[...], b_ref[...])


def _ffn_ln(x2d, layer, w_up, w_down, g, b, mixer=None):
    rows = x2d.shape[0]
    n_tiles = rows // ROW_TILE
    mixed = mixer is not None
    lag = 1 if mixed else 0
    in_tile = lambda s: jnp.minimum(s, n_tiles - 1)
    out_tile = lambda s: jnp.clip(s - lag - 1, 0, n_tiles - 1)
    row_block = lambda width: pl.BlockSpec((ROW_TILE, width), lambda s: (in_tile(s), 0))
    ffn_params = (w_up, w_down, g, b)
    in_specs = [row_block(D_MODEL)]
    operands = [x2d]
    scratch = [pltpu.VMEM((ROW_TILE, D_MODEL), jnp.float32), pltpu.VMEM((ROW_TILE, D_FF), jnp.bfloat16)]
    if mixed:
        ya, yb, yc, w_out, g2, b2 = mixer
        in_specs += [row_block(D_CONV), row_block(D_FOX), row_block(D_SGU)]
        in_specs += [_layer_spec(p, layer) for p in (w_out, g2, b2)]
        operands += [ya, yb, yc, w_out, g2, b2]
        scratch.append(pltpu.VMEM((ROW_TILE, D_MODEL), jnp.float32))
    in_specs += [_layer_spec(p, layer) for p in ffn_params]
    operands += list(ffn_params)
    return pl.pallas_call(
        functools.partial(_ffn_ln_kernel, mixed=mixed),
        grid=(n_tiles + 1 + lag,),
        in_specs=in_specs,
        out_specs=pl.BlockSpec((ROW_TILE, D_MODEL), lambda s: (out_tile(s), 0)),
        out_shape=jax.ShapeDtypeStruct((rows, D_MODEL), jnp.float32),
        scratch_shapes=scratch,
        compiler_params=pltpu.CompilerParams(
            dimension_semantics=("arbitrary",), vmem_limit_bytes=VMEM_LIMIT_BYTES),
        name="mix_ffn_ln" if mixed else "ffn_ln",
    )(*operands)


def _log_sigmoid(x):
    return -(jnp.maximum(-x, 0.0) + jnp.log1p(jnp.exp(-jnp.abs(x))))


def _cumsum_rows(y):
    n = y.shape[0]
    row = lax.broadcasted_iota(jnp.int32, y.shape, 0)
    shift = 1
    while shift < n:
        y = y + jnp.where(row >= shift, pltpu.roll(y, shift, axis=0), 0.0)
        shift *= 2
    return y


def _mixer_in_kernel(x_ref, wall_ref, bf_ref, convw_ref, sg_ref, sb_ref, wsp_ref, bsp_ref,
                     ya_ref, yc_ref, q_ref, k_ref, v_ref, cum_ref, kb_ref,
                     zbuf, fcarry):
    j = pl.program_id(1)
    tm = x_ref.shape[1]

    @pl.when(j == 0)
    def _():
        zbuf[0:CONV_HALO, :] = jnp.zeros((CONV_HALO, D_CONV), jnp.float32)
        fcarry[...] = jnp.zeros(fcarry.shape, jnp.float32)

    xb = x_ref[0].astype(jnp.bfloat16)
    proj = jnp.dot(xb, wall_ref[...], preferred_element_type=jnp.float32)
    o_f = 2 * D_SGU
    o_c = o_f + LANES
    o_qkv = o_c + 3 * D_CONV

    ps = proj[:, 0:o_f]
    u = jax.nn.gelu(ps[:, 0:D_SGU])
    vn = _layer_norm(jax.nn.gelu(ps[:, D_SGU:2 * D_SGU]), sg_ref[...], sb_ref[...]).astype(jnp.bfloat16)
    r = lax.broadcasted_iota(jnp.int32, (SGU_CHUNK, SGU_CHUNK), 0)
    c = lax.broadcasted_iota(jnp.int32, (SGU_CHUNK, SGU_CHUNK), 1)
    causal = c <= r
    lane = lax.broadcasted_iota(jnp.int32, (SGU_CHUNK, LANES), 1)
    first_group = lane < SGU_GROUP_DIM
    bias = bsp_ref[...]
    for half in range(D_SGU // LANES):
        w_pair = jnp.concatenate(
            [jnp.where(causal, wsp_ref[2 * half], 0.0), jnp.where(causal, wsp_ref[2 * half + 1], 0.0)],
            axis=0).astype(jnp.bfloat16)
        for n in range(tm // SGU_CHUNK):
            rows = slice(n * SGU_CHUNK, (n + 1) * SGU_CHUNK)
            cols = slice(half * LANES, (half + 1) * LANES)
            both = jnp.dot(w_pair, vn[rows, cols], preferred_element_type=jnp.float32)
            gated = jnp.where(first_group, both[0:SGU_CHUNK], both[SGU_CHUNK:2 * SGU_CHUNK]) + bias[:, cols]
            yc_ref[0, rows, cols] = (u[rows, cols] * gated).astype(yc_ref.dtype)

    f_logit = proj[:, o_f:o_c] + bf_ref[...]
    cum = _cumsum_rows(_log_sigmoid(f_logit)) + fcarry[...]
    fcarry[...] = cum[tm - 1:tm, :]
    cum = cum * LOG2E
    cum_ref[0] = cum
    lane = lax.broadcasted_iota(jnp.int32, cum.shape, 1)
    c0 = jnp.where(lane < N_FOX_HEADS, cum, 0.0)
    hi = c0.astype(jnp.bfloat16).astype(jnp.float32)
    r1 = c0 - hi
    mid = r1.astype(jnp.bfloat16).astype(jnp.float32)
    lo = (r1 - mid).astype(jnp.bfloat16).astype(jnp.float32)
    pieces = hi + pltpu.roll(mid, N_FOX_HEADS, axis=1) + pltpu.roll(lo, 2 * N_FOX_HEADS, axis=1)
    kb_ref[0] = (-pieces).astype(kb_ref.dtype)

    pc = proj[:, o_c:o_qkv]
    gate_b = pc[:, 0:D_CONV]
    z = pc[:, D_CONV:2 * D_CONV] * pc[:, 2 * D_CONV:3 * D_CONV]
    zbuf[CONV_HALO:CONV_HALO + tm, :] = z
    z1 = zbuf[CONV_HALO - 1:CONV_HALO - 1 + tm, :]
    z2 = zbuf[CONV_HALO - 2:CONV_HALO - 2 + tm, :]
    w = convw_ref[...]
    conv = w[0:1, :] * z2 + w[1:2, :] * z1 + w[2:3, :] * z
    ya_ref[0] = (gate_b * conv).astype(ya_ref.dtype)
    zbuf[0:CONV_HALO, :] = z[tm - CONV_HALO:tm, :]

    pqkv = proj[:, o_qkv:o_qkv + 3 * D_FOX]
    q_ref[0] = (pqkv[:, 0:D_FOX] * (FOX_HEAD_DIM ** -0.5 * LOG2E)).astype(q_ref.dtype)
    k_ref[0] = pqkv[:, D_FOX:2 * D_FOX].astype(k_ref.dtype)
    v_ref[0] = pqkv[:, 2 * D_FOX:3 * D_FOX].astype(v_ref.dtype)


def _mixer_in(x3d, layer, wall, bf, conv_w, sg, sb, w_sp, b_sp):
    bsz, seq, _ = x3d.shape
    tm = MIXER_TILE
    row_block = lambda width: pl.BlockSpec((1, tm, width), lambda b, j: (b, j, 0))
    bf16 = jnp.bfloat16
    params = (wall, bf, conv_w, sg, sb, w_sp, b_sp)
    return pl.pallas_call(
        _mixer_in_kernel,
        grid=(bsz, seq // tm),
        in_specs=[row_block(D_MODEL)] + [_layer_spec(p, layer) for p in params],
        out_specs=[
            row_block(D_CONV), row_block(D_SGU), row_block(D_FOX), row_block(D_FOX), row_block(D_FOX),
            row_block(LANES), row_block(LANES),
        ],
        out_shape=[
            jax.ShapeDtypeStruct((bsz, seq, D_CONV), bf16),
            jax.ShapeDtypeStruct((bsz, seq, D_SGU), bf16),
            jax.ShapeDtypeStruct((bsz, seq, D_FOX), bf16),
            jax.ShapeDtypeStruct((bsz, seq, D_FOX), bf16),
            jax.ShapeDtypeStruct((bsz, seq, D_FOX), bf16),
            jax.ShapeDtypeStruct((bsz, seq, LANES), jnp.float32),
            jax.ShapeDtypeStruct((bsz, seq, LANES), bf16),
        ],
        scratch_shapes=[
            pltpu.VMEM((CONV_HALO + tm, D_CONV), jnp.float32),
            pltpu.VMEM((1, LANES), jnp.float32),
        ],
        compiler_params=pltpu.CompilerParams(
            dimension_semantics=("arbitrary", "arbitrary"), vmem_limit_bytes=VMEM_LIMIT_BYTES),
        name="mixer_in",
    )(x3d, *params)


N_PAIRS = D_FOX // LANES


def _fox_attn_kernel(q_ref, k_ref, v_ref, cum_ref, kb_ref, o_ref, qs_sc, cq_sc, m_sc, l_sc, acc_sc, s_sc, mx_sc):
    i = pl.program_id(1)
    tq = q_ref.shape[1]
    tk = tq
    lane = lax.broadcasted_iota(jnp.int32, (tq, LANES), 1)
    first_head = lane < FOX_HEAD_DIM

    cum_t = cum_ref[0].T
    for p in range(N_PAIRS):
        q2 = q_ref[0, :, p * LANES:(p + 1) * LANES]
        zero = jnp.zeros_like(q2)
        qs_sc[p, 0:tq, 0:LANES] = jnp.where(first_head, q2, zero)
        qs_sc[p, tq:2 * tq, 0:LANES] = jnp.where(first_head, zero, q2)
        for hh in range(2):
            h = 2 * p + hh
            cq_sc[p, :, hh * tq:(hh + 1) * tq] = cum_t[h:h + 1, :]
            pick = (lane == h) | (lane == N_FOX_HEADS + h) | (lane == 2 * N_FOX_HEADS + h)
            qs_sc[p, hh * tq:(hh + 1) * tq, LANES:2 * LANES] = jnp.where(pick, 1.0, 0.0).astype(qs_sc.dtype)
    m_sc[...] = jnp.full(m_sc.shape, -jnp.inf, jnp.float32)
    l_sc[...] = jnp.zeros(l_sc.shape, jnp.float32)
    acc_sc[...] = jnp.zeros(acc_sc.shape, jnp.float32)

    key = lax.broadcasted_iota(jnp.int32, (tk, 2 * tq), 0)
    qry = lax.broadcasted_iota(jnp.int32, (tk, 2 * tq), 1)
    qry = jnp.where(qry >= tq, qry - tq, qry)

    def step(j, diagonal):
        start = pl.multiple_of(j * tk, tk)

        def scores(p):
            k2 = jnp.concatenate(
                [k_ref[0, pl.ds(start, tk), p * LANES:(p + 1) * LANES], kb_ref[0, pl.ds(start, tk), :]], axis=1)
            s = lax.dot_general(k2, qs_sc[p], (((1,), (1,)), ((), ())),
                                preferred_element_type=jnp.float32)
            if diagonal:
                s = jnp.where(key <= qry, s, -jnp.inf)
            s_sc[p % 2] = s
            mx_sc[p % 2] = jnp.max(s, axis=0, keepdims=True)

        scores(0)
        for p in range(N_PAIRS):
            if p + 1 < N_PAIRS:
                scores(p + 1)
            s = s_sc[p % 2]
            v2 = v_ref[0, pl.ds(start, tk), p * LANES:(p + 1) * LANES]
            cq = cq_sc[p]
            m_old = m_sc[p]
            m_new = jnp.maximum(m_old, mx_sc[p % 2] + cq)
            scale = jnp.exp2(m_old - m_new)
            probs = jnp.exp2(s - (m_new - cq))
            l_sc[p] = scale * l_sc[p] + jnp.sum(probs, axis=0, keepdims=True)
            pv = lax.dot_general(v2, probs.astype(jnp.bfloat16), (((0,), (0,)), ((), ())),
                                 preferred_element_type=jnp.float32)
            acc_sc[p] = scale * acc_sc[p] + pv
            m_sc[p] = m_new

    def body(j, carry):
        step(j, diagonal=False)
        return carry

    lax.fori_loop(0, i, body, 0)
    step(i, diagonal=True)

    dim = lax.broadcasted_iota(jnp.int32, (LANES, tq), 0)
    for p in range(N_PAIRS):
        o = acc_sc[p] / l_sc[p]
        o = jnp.where(dim < FOX_HEAD_DIM, o[:, 0:tq], o[:, tq:2 * tq])
        o_ref[0, :, p * LANES:(p + 1) * LANES] = o.T.astype(o_ref.dtype)


def _fox_attn(q, k, v, cum, kb):
    bsz, seq, _ = q.shape
    tq = ATTN_BLOCK
    stat = pltpu.VMEM((N_PAIRS, 1, 2 * tq), jnp.float32)
    return pl.pallas_call(
        _fox_attn_kernel,
        grid=(bsz, seq // tq),
        in_specs=[
            pl.BlockSpec((1, tq, D_FOX), lambda b, i: (b, i, 0)),
            pl.BlockSpec((1, seq, D_FOX), lambda b, i: (b, 0, 0)),
            pl.BlockSpec((1, seq, D_FOX), lambda b, i: (b, 0, 0)),
            pl.BlockSpec((1, tq, LANES), lambda b, i: (b, i, 0)),
            pl.BlockSpec((1, seq, LANES), lambda b, i: (b, 0, 0)),
        ],
        out_specs=pl.BlockSpec((1, tq, D_FOX), lambda b, i: (b, i, 0)),
        out_shape=jax.ShapeDtypeStruct((bsz, seq, D_FOX), jnp.bfloat16),
        scratch_shapes=[pltpu.VMEM((N_PAIRS, 2 * tq, 2 * LANES), jnp.bfloat16), stat, stat, stat,
                        pltpu.VMEM((N_PAIRS, LANES, 2 * tq), jnp.float32),
                        pltpu.VMEM((2, tq, 2 * tq), jnp.float32), pltpu.VMEM((2, 1, 2 * tq), jnp.float32)],
        compiler_params=pltpu.CompilerParams(
            dimension_semantics=("arbitrary", "arbitrary"), vmem_limit_bytes=VMEM_LIMIT_BYTES),
        name="fox_attn",
    )(q, k, v, cum, kb)


def kernel(x, ln1_g, ln1_b, ffn1_w_up, ffn1_w_down, mix_w_in, fox_b_f, conv_w, sgu_ln_g, sgu_ln_b, sgu_w_s,
           sgu_b_s, mix_w_out, ln2_g, ln2_b, ffn2_w_up, ffn2_w_down, ln3_g, ln3_b):
    bsz, seq, d = x.shape
    assert d == D_MODEL and seq % MIXER_TILE == 0 and seq % ATTN_BLOCK == 0 and (bsz * seq) % ROW_TILE == 0
    bf16 = jnp.bfloat16
    rows = bsz * seq
    row_vecs = lambda p: p.reshape(p.shape[0], 1, -1)
    o_qkv = 3 * D_CONV
    o_f = o_qkv + 3 * D_FOX
    o_s = o_f + N_FOX_HEADS
    pad_f = LANES - N_FOX_HEADS

    w_up1, w_dn1 = ffn1_w_up.astype(bf16), ffn1_w_down.astype(bf16)
    w_up2, w_dn2 = ffn2_w_up.astype(bf16), ffn2_w_down.astype(bf16)
    wall = jnp.concatenate(
        [mix_w_in[:, :, o_s:o_s + 2 * D_SGU], jnp.pad(mix_w_in[:, :, o_f:o_s], ((0, 0), (0, 0), (0, pad_f))),
         mix_w_in[:, :, 0:o_f]], axis=2).astype(bf16)
    b_f = jnp.pad(fox_b_f, ((0, 0), (0, pad_f))).reshape(DEPTH, 1, LANES)
    b_sp = jnp.repeat(jnp.swapaxes(sgu_b_s, 1, 2), SGU_GROUP_DIM, axis=2)
    w_out = mix_w_out.astype(bf16)
    g1, b1, g2, b2, g3, b3 = (row_vecs(p) for p in (ln1_g, ln1_b, ln2_g, ln2_b, ln3_g, ln3_b))
    sg, sb = row_vecs(sgu_ln_g), row_vecs(sgu_ln_b)

    h = x.reshape(rows, d)
    for l in range(DEPTH):
        h = _ffn_ln(h, l, w_up1, w_dn1, g1, b1)
        ya, yc, q, k, v, cum, kb = _mixer_in(h.reshape(bsz, seq, d), l, wall, b_f, conv_w, sg, sb, sgu_w_s, b_sp)
        yb = _fox_attn(q, k, v, cum, kb)
        mixer = (ya.reshape(rows, D_CONV), yb.reshape(rows, D_FOX), yc.reshape(rows, D_SGU), w_out, g2, b2)
        h = _ffn_ln(h, l, w_up2, w_dn2, g3, b3, mixer=mixer)
    return h.reshape(bsz, seq, d)
```

```python
import functools

import jax
import jax.numpy as jnp
from jax import lax
from jax.experimental import pallas as pl
from jax.experimental.pallas import tpu as pltpu

D_MODEL = 1024
DEPTH = 2
D_CONV = 256
CONV_WIDTH = 3
D_FOX = 512
FOX_HEAD_DIM = 64
N_FOX_HEADS = 8
D_SGU = 256
N_SGU_GROUPS = 4
SGU_GROUP_DIM = 64
SGU_CHUNK = 128
D_FF = 2816
ALPHA = (2 * DEPTH) ** 0.25
LN_EPS = 1e-5
LOG2E = 1.4426950408889634

LANES = 128
CONV_HALO = 8
VMEM_LIMIT_BYTES = 56 * 1024 * 1024

ROW_TILE = 512
MIXER_TILE = 1024
FF_CHUNK = 256
ATTN_BLOCK = 512


def _layer_norm(h, g, b):
    mu = jnp.mean(h, axis=-1, keepdims=True)
    hc = h - mu
    var = jnp.mean(hc * hc, axis=-1, keepdims=True)
    return hc * lax.rsqrt(var + LN_EPS) * g + b


def _layer_spec(stacked, layer):
    zeros = (0,) * (stacked.ndim - 1)
    return pl.BlockSpec((None,) + stacked.shape[1:], lambda *_: (layer,) + zeros, pipeline_mode=pl.Buffered(1))


def _ffn_ln_kernel(*refs, mixed):
    if mixed:
        (x_ref, ya_ref, yb_ref, yc_ref, wo_ref, g2_ref, b2_ref,
         wup_ref, wdn_ref, g_ref, b_ref, o_ref, pre_sc, act_sc, h_sc) = refs
    else:
        x_ref, wup_ref, wdn_ref, g_ref, b_ref, o_ref, pre_sc, act_sc = refs
    lag = 1 if mixed else 0
    s = pl.program_id(0)
    last = pl.num_programs(0) - 1

    def mixer_out():
        y = jnp.concatenate([ya_ref[...], yb_ref[...], yc_ref[...]], axis=1)
        mix = jnp.dot(y, wo_ref[...], preferred_element_type=jnp.float32)
        return _layer_norm(ALPHA * x_ref[...] + mix, g2_ref[...], b2_ref[...])

    @pl.when(s == 0)
    def _():
        pre_sc[...] = jnp.zeros(pre_sc.shape, jnp.float32)
        if mixed:
            h_sc[...] = mixer_out()

    @pl.when((s >= lag) & (s < last))
    def _():
        done = _layer_norm(pre_sc[...], g_ref[...], b_ref[...])
        o_ref[...] = done
        h = h_sc[...] if mixed else x_ref[...]
        h_next = mixer_out() if mixed else None
        hb = h.astype(jnp.bfloat16)
        never = s < 0
        n_anchor = D_MODEL // FF_CHUNK
        for c in range(D_FF // FF_CHUNK):
            lo = c * FF_CHUNK
            gate = jnp.dot(hb, wup_ref[:, lo:lo + FF_CHUNK], preferred_element_type=jnp.float32)
            up = jnp.dot(hb, wup_ref[:, D_FF + lo:D_FF + lo + FF_CHUNK], preferred_element_type=jnp.float32)
            act = (jax.nn.silu(gate) * up).astype(jnp.bfloat16)
            if 1 <= c <= n_anchor:
                cols = slice((c - 1) * FF_CHUNK, c * FF_CHUNK)
                act = jnp.where(never, done[:, cols].astype(jnp.bfloat16), act)
            elif mixed and n_anchor + 2 <= c < 2 * n_anchor + 2:
                cols = slice((c - n_anchor - 2) * FF_CHUNK, (c - n_anchor - 1) * FF_CHUNK)
                act = jnp.where(never, h_next[:, cols].astype(jnp.bfloat16), act)
            act_sc[:, lo:lo + FF_CHUNK] = act
        acc = jnp.dot(act_sc[...], wdn_ref[...], preferred_element_type=jnp.float32)
        pre_sc[...] = ALPHA * h + 0.5 * acc
        if mixed:
            h_sc[...] = h_next

    @pl.when(s == last)
    def _():
        o_ref[...] = _layer_norm(pre_sc[...], g_ref[...], b_ref[...])


def _ffn_ln(x2d, layer, w_up, w_down, g, b, mixer=None):
    rows = x2d.shape[0]
    n_tiles = rows // ROW_TILE
    mixed = mixer is not None
    lag = 1 if mixed else 0
    in_tile = lambda s: jnp.minimum(s, n_tiles - 1)
    out_tile = lambda s: jnp.clip(s - lag - 1, 0, n_tiles - 1)
    row_block = lambda width: pl.BlockSpec((ROW_TILE, width), lambda s: (in_tile(s), 0))
    ffn_params = (w_up, w_down, g, b)
    in_specs = [row_block(D_MODEL)]
    operands = [x2d]
    scratch = [pltpu.VMEM((ROW_TILE, D_MODEL), jnp.float32), pltpu.VMEM((ROW_TILE, D_FF), jnp.bfloat16)]
    if mixed:
        ya, yb, yc, w_out, g2, b2 = mixer
        in_specs += [row_block(D_CONV), row_block(D_FOX), row_block(D_SGU)]
        in_specs += [_layer_spec(p, layer) for p in (w_out, g2, b2)]
        operands += [ya, yb, yc, w_out, g2, b2]
        scratch.append(pltpu.VMEM((ROW_TILE, D_MODEL), jnp.float32))
    in_specs += [_layer_spec(p, layer) for p in ffn_params]
    operands += list(ffn_params)
    return pl.pallas_call(
        functools.partial(_ffn_ln_kernel, mixed=mixed),
        grid=(n_tiles + 1 + lag,),
        in_specs=in_specs,
        out_specs=pl.BlockSpec((ROW_TILE, D_MODEL), lambda s: (out_tile(s), 0)),
        out_shape=jax.ShapeDtypeStruct((rows, D_MODEL), jnp.float32),
        scratch_shapes=scratch,
        compiler_params=pltpu.CompilerParams(
            dimension_semantics=("arbitrary",), vmem_limit_bytes=VMEM_LIMIT_BYTES),
        name="mix_ffn_ln" if mixed else "ffn_ln",
    )(*operands)


def _log_sigmoid(x):
    return -(jnp.maximum(-x, 0.0) + jnp.log1p(jnp.exp(-jnp.abs(x))))


def _cumsum_rows(y):
    n = y.shape[0]
    row = lax.broadcasted_iota(jnp.int32, y.shape, 0)
    shift = 1
    while shift < n:
        y = y + jnp.where(row >= shift, pltpu.roll(y, shift, axis=0), 0.0)
        shift *= 2
    return y


def _mixer_in_kernel(x_ref, wall_ref, bf_ref, convw_ref, sg_ref, sb_ref, wsp_ref, bsp_ref,
                     ya_ref, yc_ref, q_ref, k_ref, v_ref, cum_ref, kb_ref,
                     zbuf, fcarry):
    j = pl.program_id(1)
    tm = x_ref.shape[1]

    @pl.when(j == 0)
    def _():
        zbuf[0:CONV_HALO, :] = jnp.zeros((CONV_HALO, D_CONV), jnp.float32)
        fcarry[...] = jnp.zeros(fcarry.shape, jnp.float32)

    xb = x_ref[0].astype(jnp.bfloat16)
    proj = jnp.dot(xb, wall_ref[...], preferred_element_type=jnp.float32)
    o_f = 2 * D_SGU
    o_c = o_f + LANES
    o_qkv = o_c + 3 * D_CONV

    ps = proj[:, 0:o_f]
    u = jax.nn.gelu(ps[:, 0:D_SGU])
    vn = _layer_norm(jax.nn.gelu(ps[:, D_SGU:2 * D_SGU]), sg_ref[...], sb_ref[...]).astype(jnp.bfloat16)
    r = lax.broadcasted_iota(jnp.int32, (SGU_CHUNK, SGU_CHUNK), 0)
    c = lax.broadcasted_iota(jnp.int32, (SGU_CHUNK, SGU_CHUNK), 1)
    causal = c <= r
    lane = lax.broadcasted_iota(jnp.int32, (SGU_CHUNK, LANES), 1)
    first_group = lane < SGU_GROUP_DIM
    bias = bsp_ref[...]
    for half in range(D_SGU // LANES):
        w_pair = jnp.concatenate(
            [jnp.where(causal, wsp_ref[2 * half], 0.0), jnp.where(causal, wsp_ref[2 * half + 1], 0.0)],
            axis=0).astype(jnp.bfloat16)
        for n in range(tm // SGU_CHUNK):
            rows = slice(n * SGU_CHUNK, (n + 1) * SGU_CHUNK)
            cols = slice(half * LANES, (half + 1) * LANES)
            both = jnp.dot(w_pair, vn[rows, cols], preferred_element_type=jnp.float32)
            gated = jnp.where(first_group, both[0:SGU_CHUNK], both[SGU_CHUNK:2 * SGU_CHUNK]) + bias[:, cols]
            yc_ref[0, rows, cols] = (u[rows, cols] * gated).astype(yc_ref.dtype)

    f_logit = proj[:, o_f:o_c] + bf_ref[...]
    cum = _cumsum_rows(_log_sigmoid(f_logit)) + fcarry[...]
    fcarry[...] = cum[tm - 1:tm, :]
    cum = cum * LOG2E
    cum_ref[0] = cum
    lane = lax.broadcasted_iota(jnp.int32, cum.shape, 1)
    c0 = jnp.where(lane < N_FOX_HEADS, cum, 0.0)
    hi = c0.astype(jnp.bfloat16).astype(jnp.float32)
    r1 = c0 - hi
    mid = r1.astype(jnp.bfloat16).astype(jnp.float32)
    lo = (r1 - mid).astype(jnp.bfloat16).astype(jnp.float32)
    pieces = hi + pltpu.roll(mid, N_FOX_HEADS, axis=1) + pltpu.roll(lo, 2 * N_FOX_HEADS, axis=1)
    kb_ref[0] = (-pieces).astype(kb_ref.dtype)

    pc = proj[:, o_c:o_qkv]
    gate_b = pc[:, 0:D_CONV]
    z = pc[:, D_CONV:2 * D_CONV] * pc[:, 2 * D_CONV:3 * D_CONV]
    zbuf[CONV_HALO:CONV_HALO + tm, :] = z
    z1 = zbuf[CONV_HALO - 1:CONV_HALO - 1 + tm, :]
    z2 = zbuf[CONV_HALO - 2:CONV_HALO - 2 + tm, :]
    w = convw_ref[...]
    conv = w[0:1, :] * z2 + w[1:2, :] * z1 + w[2:3, :] * z
    ya_ref[0] = (gate_b * conv).astype(ya_ref.dtype)
    zbuf[0:CONV_HALO, :] = z[tm - CONV_HALO:tm, :]

    pqkv = proj[:, o_qkv:o_qkv + 3 * D_FOX]
    q_ref[0] = (pqkv[:, 0:D_FOX] * (FOX_HEAD_DIM ** -0.5 * LOG2E)).astype(q_ref.dtype)
    k_ref[0] = pqkv[:, D_FOX:2 * D_FOX].astype(k_ref.dtype)
    v_ref[0] = pqkv[:, 2 * D_FOX:3 * D_FOX].astype(v_ref.dtype)


def _mixer_in(x3d, layer, wall, bf, conv_w, sg, sb, w_sp, b_sp):
    bsz, seq, _ = x3d.shape
    tm = MIXER_TILE
    row_block = lambda width: pl.BlockSpec((1, tm, width), lambda b, j: (b, j, 0))
    bf16 = jnp.bfloat16
    params = (wall, bf, conv_w, sg, sb, w_sp, b_sp)
    return pl.pallas_call(
        _mixer_in_kernel,
        grid=(bsz, seq // tm),
        in_specs=[row_block(D_MODEL)] + [_layer_spec(p, layer) for p in params],
        out_specs=[
            row_block(D_CONV), row_block(D_SGU), row_block(D_FOX), row_block(D_FOX), row_block(D_FOX),
            row_block(LANES), row_block(LANES),
        ],
        out_shape=[
            jax.ShapeDtypeStruct((bsz, seq, D_CONV), bf16),
            jax.ShapeDtypeStruct((bsz, seq, D_SGU), bf16),
            jax.ShapeDtypeStruct((bsz, seq, D_FOX), bf16),
            jax.ShapeDtypeStruct((bsz, seq, D_FOX), bf16),
            jax.ShapeDtypeStruct((bsz, seq, D_FOX), bf16),
            jax.ShapeDtypeStruct((bsz, seq, LANES), jnp.float32),
            jax.ShapeDtypeStruct((bsz, seq, LANES), bf16),
        ],
        scratch_shapes=[
            pltpu.VMEM((CONV_HALO + tm, D_CONV), jnp.float32),
            pltpu.VMEM((1, LANES), jnp.float32),
        ],
        compiler_params=pltpu.CompilerParams(
            dimension_semantics=("arbitrary", "arbitrary"), vmem_limit_bytes=VMEM_LIMIT_BYTES),
        name="mixer_in",
    )(x3d, *params)


N_PAIRS = D_FOX // LANES


def _fox_attn_kernel(q_ref, k_ref, v_ref, cum_ref, kb_ref, o_ref, qs_sc, cq_sc, m_sc, l_sc, acc_sc, s_sc):
    i = pl.program_id(1)
    tq = q_ref.shape[1]
    tk = tq
    lane = lax.broadcasted_iota(jnp.int32, (tq, LANES), 1)
    first_head = lane < FOX_HEAD_DIM

    cum_t = cum_ref[0].T
    for p in range(N_PAIRS):
        q2 = q_ref[0, :, p * LANES:(p + 1) * LANES]
        zero = jnp.zeros_like(q2)
        heads = [jnp.where(first_head, q2, zero), jnp.where(first_head, zero, q2)]
        for hh in range(2):
            h = 2 * p + hh
            cq_sc[p, :, hh * tq:(hh + 1) * tq] = cum_t[h:h + 1, :]
            pick = (lane == h) | (lane == N_FOX_HEADS + h) | (lane == 2 * N_FOX_HEADS + h)
            ones = jnp.where(pick, 1.0, 0.0).astype(qs_sc.dtype)
            qs_sc[p, 0:LANES, hh * tq:(hh + 1) * tq] = heads[hh].T
            qs_sc[p, LANES:2 * LANES, hh * tq:(hh + 1) * tq] = ones.T
    m_sc[...] = jnp.full(m_sc.shape, -jnp.inf, jnp.float32)
    l_sc[...] = jnp.zeros(l_sc.shape, jnp.float32)
    acc_sc[...] = jnp.zeros(acc_sc.shape, jnp.float32)

    key = lax.broadcasted_iota(jnp.int32, (tk, 2 * tq), 0)
    qry = lax.broadcasted_iota(jnp.int32, (tk, 2 * tq), 1)
    qry = jnp.where(qry >= tq, qry - tq, qry)

    def step(j, diagonal):
        start = pl.multiple_of(j * tk, tk)

        def scores(p):
            k2 = jnp.concatenate(
                [k_ref[0, pl.ds(start, tk), p * LANES:(p + 1) * LANES], kb_ref[0, pl.ds(start, tk), :]], axis=1)
            s_sc[p % 2] = jnp.dot(k2, qs_sc[p], preferred_element_type=jnp.float32)

        scores(0)
        for p in range(N_PAIRS):
            if p + 1 < N_PAIRS:
                scores(p + 1)
            s = s_sc[p % 2]
            v2 = v_ref[0, pl.ds(start, tk), p * LANES:(p + 1) * LANES]
            if diagonal:
                s = jnp.where(key <= qry, s, -jnp.inf)
            cq = cq_sc[p]
            m_old = m_sc[p]
            m_new = jnp.maximum(m_old, jnp.max(s, axis=0, keepdims=True) + cq)
            scale = jnp.exp2(m_old - m_new)
            probs = jnp.exp2(s - (m_new - cq))
            l_sc[p] = scale * l_sc[p] + jnp.sum(probs, axis=0, keepdims=True)
            pv = lax.dot_general(v2, probs.astype(jnp.bfloat16), (((0,), (0,)), ((), ())),
                                 preferred_element_type=jnp.float32)
            acc_sc[p] = scale * acc_sc[p] + pv
            m_sc[p] = m_new

    def body(j, carry):
        step(j, diagonal=False)
        return carry

    lax.fori_loop(0, i, body, 0)
    step(i, diagonal=True)

    dim = lax.broadcasted_iota(jnp.int32, (LANES, tq), 0)
    for p in range(N_PAIRS):
        o = acc_sc[p] / l_sc[p]
        o = jnp.where(dim < FOX_HEAD_DIM, o[:, 0:tq], o[:, tq:2 * tq])
        o_ref[0, :, p * LANES:(p + 1) * LANES] = o.T.astype(o_ref.dtype)


def _fox_attn(q, k, v, cum, kb):
    bsz, seq, _ = q.shape
    tq = ATTN_BLOCK
    stat = pltpu.VMEM((N_PAIRS, 1, 2 * tq), jnp.float32)
    return pl.pallas_call(
        _fox_attn_kernel,
        grid=(bsz, seq // tq),
        in_specs=[
            pl.BlockSpec((1, tq, D_FOX), lambda b, i: (b, i, 0)),
            pl.BlockSpec((1, seq, D_FOX), lambda b, i: (b, 0, 0)),
            pl.BlockSpec((1, seq, D_FOX), lambda b, i: (b, 0, 0)),
            pl.BlockSpec((1, tq, LANES), lambda b, i: (b, i, 0)),
            pl.BlockSpec((1, seq, LANES), lambda b, i: (b, 0, 0)),
        ],
        out_specs=pl.BlockSpec((1, tq, D_FOX), lambda b, i: (b, i, 0)),
        out_shape=jax.ShapeDtypeStruct((bsz, seq, D_FOX), jnp.bfloat16),
        scratch_shapes=[pltpu.VMEM((N_PAIRS, 2 * LANES, 2 * tq), jnp.bfloat16), stat, stat, stat,
                        pltpu.VMEM((N_PAIRS, LANES, 2 * tq), jnp.float32),
                        pltpu.VMEM((2, tq, 2 * tq), jnp.float32)],
        compiler_params=pltpu.CompilerParams(
            dimension_semantics=("arbitrary", "arbitrary"), vmem_limit_bytes=VMEM_LIMIT_BYTES),
        name="fox_attn",
    )(q, k, v, cum, kb)


def kernel(x, ln1_g, ln1_b, ffn1_w_up, ffn1_w_down, mix_w_in, fox_b_f, conv_w, sgu_ln_g, sgu_ln_b, sgu_w_s,
           sgu_b_s, mix_w_out, ln2_g, ln2_b, ffn2_w_up, ffn2_w_down, ln3_g, ln3_b):
    bsz, seq, d = x.shape
    assert d == D_MODEL and seq % MIXER_TILE == 0 and seq % ATTN_BLOCK == 0 and (bsz * seq) % ROW_TILE == 0
    bf16 = jnp.bfloat16
    rows = bsz * seq
    row_vecs = lambda p: p.reshape(p.shape[0], 1, -1)
    o_qkv = 3 * D_CONV
    o_f = o_qkv + 3 * D_FOX
    o_s = o_f + N_FOX_HEADS
    pad_f = LANES - N_FOX_HEADS

    w_up1, w_dn1 = ffn1_w_up.astype(bf16), ffn1_w_down.astype(bf16)
    w_up2, w_dn2 = ffn2_w_up.astype(bf16), ffn2_w_down.astype(bf16)
    wall = jnp.concatenate(
        [mix_w_in[:, :, o_s:o_s + 2 * D_SGU], jnp.pad(mix_w_in[:, :, o_f:o_s], ((0, 0), (0, 0), (0, pad_f))),
         mix_w_in[:, :, 0:o_f]], axis=2).astype(bf16)
    b_f = jnp.pad(fox_b_f, ((0, 0), (0, pad_f))).reshape(DEPTH, 1, LANES)
    b_sp = jnp.repeat(jnp.swapaxes(sgu_b_s, 1, 2), SGU_GROUP_DIM, axis=2)
    w_out = mix_w_out.astype(bf16)
    g1, b1, g2, b2, g3, b3 = (row_vecs(p) for p in (ln1_g, ln1_b, ln2_g, ln2_b, ln3_g, ln3_b))
    sg, sb = row_vecs(sgu_ln_g), row_vecs(sgu_ln_b)

    h = x.reshape(rows, d)
    for l in range(DEPTH):
        h = _ffn_ln(h, l, w_up1, w_dn1, g1, b1)
        ya, yc, q, k, v, cum, kb = _mixer_in(h.reshape(bsz, seq, d), l, wall, b_f, conv_w, sg, sb, sgu_w_s, b_sp)
        yb = _fox_attn(q, k, v, cum, kb)
        mixer = (ya.reshape(rows, D_CONV), yb.reshape(rows, D_FOX), yc.reshape(rows, D_SGU), w_out, g2, b2)
        h = _ffn_ln(h, l, w_up2, w_dn2, g3, b3, mixer=mixer)
    return h.reshape(bsz, seq, d)
```

```python
import functools

import jax
import jax.numpy as jnp
from jax import lax
from jax.experimental import pallas as pl
from jax.experimental.pallas import tpu as pltpu

D_MODEL = 1024
DEPTH = 2
D_CONV = 256
CONV_WIDTH = 3
D_FOX = 512
FOX_HEAD_DIM = 64
N_FOX_HEADS = 8
D_SGU = 256
N_SGU_GROUPS = 4
SGU_GROUP_DIM = 64
SGU_CHUNK = 128
D_FF = 2816
ALPHA = (2 * DEPTH) ** 0.25
LN_EPS = 1e-5
LOG2E = 1.4426950408889634

LANES = 128
CONV_HALO = 8
VMEM_LIMIT_BYTES = 56 * 1024 * 1024

ROW_TILE = 512
MIXER_TILE = 1024
FF_CHUNK = 256
ATTN_BLOCK = 512


def _layer_norm(h, g, b):
    mu = jnp.mean(h, axis=-1, keepdims=True)
    hc = h - mu
    var = jnp.mean(hc * hc, axis=-1, keepdims=True)
    return hc * lax.rsqrt(var + LN_EPS) * g + b


def _layer_spec(stacked, layer):
    zeros = (0,) * (stacked.ndim - 1)
    return pl.BlockSpec((None,) + stacked.shape[1:], lambda *_: (layer,) + zeros, pipeline_mode=pl.Buffered(1))


def _ffn_ln_kernel(*refs, mixed):
    if mixed:
        (x_ref, ya_ref, yb_ref, yc_ref, wo_ref, g2_ref, b2_ref,
         wup_ref, wdn_ref, g_ref, b_ref, o_ref, pre_sc, act_sc, h_sc) = refs
    else:
        x_ref, wup_ref, wdn_ref, g_ref, b_ref, o_ref, pre_sc, act_sc = refs
    lag = 1 if mixed else 0
    s = pl.program_id(0)
    last = pl.num_programs(0) - 1

    def mixer_out():
        y = jnp.concatenate([ya_ref[...], yb_ref[...], yc_ref[...]], axis=1)
        mix = jnp.dot(y, wo_ref[...], preferred_element_type=jnp.float32)
        return _layer_norm(ALPHA * x_ref[...] + mix, g2_ref[...], b2_ref[...])

    @pl.when(s == 0)
    def _():
        pre_sc[...] = jnp.zeros(pre_sc.shape, jnp.float32)
        if mixed:
            h_sc[...] = mixer_out()

    @pl.when((s >= lag) & (s < last))
    def _():
        done = _layer_norm(pre_sc[...], g_ref[...], b_ref[...])
        o_ref[...] = done
        h = h_sc[...] if mixed else x_ref[...]
        h_next = mixer_out() if mixed else None
        hb = h.astype(jnp.bfloat16)
        never = s < 0
        n_anchor = D_MODEL // FF_CHUNK
        for c in range(D_FF // FF_CHUNK):
            lo = c * FF_CHUNK
            gate = jnp.dot(hb, wup_ref[:, lo:lo + FF_CHUNK], preferred_element_type=jnp.float32)
            up = jnp.dot(hb, wup_ref[:, D_FF + lo:D_FF + lo + FF_CHUNK], preferred_element_type=jnp.float32)
            act = (jax.nn.silu(gate) * up).astype(jnp.bfloat16)
            if 1 <= c <= n_anchor:
                cols = slice((c - 1) * FF_CHUNK, c * FF_CHUNK)
                act = jnp.where(never, done[:, cols].astype(jnp.bfloat16), act)
            elif mixed and n_anchor + 2 <= c < 2 * n_anchor + 2:
                cols = slice((c - n_anchor - 2) * FF_CHUNK, (c - n_anchor - 1) * FF_CHUNK)
                act = jnp.where(never, h_next[:, cols].astype(jnp.bfloat16), act)
            act_sc[:, lo:lo + FF_CHUNK] = act
        acc = jnp.dot(act_sc[...], wdn_ref[...], preferred_element_type=jnp.float32)
        pre_sc[...] = ALPHA * h + 0.5 * acc
        if mixed:
            h_sc[...] = h_next

    @pl.when(s == last)
    def _():
        o_ref[...] = _layer_norm(pre_sc[...], g_ref[...], b_ref[...])


def _ffn_ln(x2d, layer, w_up, w_down, g, b, mixer=None):
    rows = x2d.shape[0]
    n_tiles = rows // ROW_TILE
    mixed = mixer is not None
    lag = 1 if mixed else 0
    in_tile = lambda s: jnp.minimum(s, n_tiles - 1)
    out_tile = lambda s: jnp.clip(s - lag - 1, 0, n_tiles - 1)
    row_block = lambda width: pl.BlockSpec((ROW_TILE, width), lambda s: (in_tile(s), 0))
    ffn_params = (w_up, w_down, g, b)
    in_specs = [row_block(D_MODEL)]
    operands = [x2d]
    scratch = [pltpu.VMEM((ROW_TILE, D_MODEL), jnp.float32), pltpu.VMEM((ROW_TILE, D_FF), jnp.bfloat16)]
    if mixed:
        ya, yb, yc, w_out, g2, b2 = mixer
        in_specs += [row_block(D_CONV), row_block(D_FOX), row_block(D_SGU)]
        in_specs += [_layer_spec(p, layer) for p in (w_out, g2, b2)]
        operands += [ya, yb, yc, w_out, g2, b2]
        scratch.append(pltpu.VMEM((ROW_TILE, D_MODEL), jnp.float32))
    in_specs += [_layer_spec(p, layer) for p in ffn_params]
    operands += list(ffn_params)
    return pl.pallas_call(
        functools.partial(_ffn_ln_kernel, mixed=mixed),
        grid=(n_tiles + 1 + lag,),
        in_specs=in_specs,
        out_specs=pl.BlockSpec((ROW_TILE, D_MODEL), lambda s: (out_tile(s), 0)),
        out_shape=jax.ShapeDtypeStruct((rows, D_MODEL), jnp.float32),
        scratch_shapes=scratch,
        compiler_params=pltpu.CompilerParams(
            dimension_semantics=("arbitrary",), vmem_limit_bytes=VMEM_LIMIT_BYTES),
        name="mix_ffn_ln" if mixed else "ffn_ln",
    )(*operands)


def _log_sigmoid(x):
    return -(jnp.maximum(-x, 0.0) + jnp.log1p(jnp.exp(-jnp.abs(x))))


def _cumsum_rows(y):
    n = y.shape[0]
    row = lax.broadcasted_iota(jnp.int32, y.shape, 0)
    shift = 1
    while shift < n:
        y = y + jnp.where(row >= shift, pltpu.roll(y, shift, axis=0), 0.0)
        shift *= 2
    return y


def _mixer_in_kernel(x_ref, wall_ref, bf_ref, convw_ref, sg_ref, sb_ref, wsp_ref, bsp_ref,
                     ya_ref, yc_ref, q_ref, k_ref, v_ref, cum_ref, kb_ref,
                     zbuf, fcarry):
    j = pl.program_id(1)
    tm = x_ref.shape[1]

    @pl.when(j == 0)
    def _():
        zbuf[0:CONV_HALO, :] = jnp.zeros((CONV_HALO, D_CONV), jnp.float32)
        fcarry[...] = jnp.zeros(fcarry.shape, jnp.float32)

    xb = x_ref[0].astype(jnp.bfloat16)
    proj = jnp.dot(xb, wall_ref[...], preferred_element_type=jnp.float32)
    o_f = 2 * D_SGU
    o_c = o_f + LANES
    o_qkv = o_c + 3 * D_CONV

    ps = proj[:, 0:o_f]
    u = jax.nn.gelu(ps[:, 0:D_SGU])
    vn = _layer_norm(jax.nn.gelu(ps[:, D_SGU:2 * D_SGU]), sg_ref[...], sb_ref[...]).astype(jnp.bfloat16)
    r = lax.broadcasted_iota(jnp.int32, (SGU_CHUNK, SGU_CHUNK), 0)
    c = lax.broadcasted_iota(jnp.int32, (SGU_CHUNK, SGU_CHUNK), 1)
    causal = c <= r
    lane = lax.broadcasted_iota(jnp.int32, (SGU_CHUNK, LANES), 1)
    first_group = lane < SGU_GROUP_DIM
    bias = bsp_ref[...]
    for half in range(D_SGU // LANES):
        w_pair = jnp.concatenate(
            [jnp.where(causal, wsp_ref[2 * half], 0.0), jnp.where(causal, wsp_ref[2 * half + 1], 0.0)],
            axis=0).astype(jnp.bfloat16)
        for n in range(tm // SGU_CHUNK):
            rows = slice(n * SGU_CHUNK, (n + 1) * SGU_CHUNK)
            cols = slice(half * LANES, (half + 1) * LANES)
            both = jnp.dot(w_pair, vn[rows, cols], preferred_element_type=jnp.float32)
            gated = jnp.where(first_group, both[0:SGU_CHUNK], both[SGU_CHUNK:2 * SGU_CHUNK]) + bias[:, cols]
            yc_ref[0, rows, cols] = (u[rows, cols] * gated).astype(yc_ref.dtype)

    f_logit = proj[:, o_f:o_c] + bf_ref[...]
    cum = _cumsum_rows(_log_sigmoid(f_logit)) + fcarry[...]
    fcarry[...] = cum[tm - 1:tm, :]
    cum = cum * LOG2E
    cum_ref[0] = cum
    lane = lax.broadcasted_iota(jnp.int32, cum.shape, 1)
    c0 = jnp.where(lane < N_FOX_HEADS, cum, 0.0)
    hi = c0.astype(jnp.bfloat16).astype(jnp.float32)
    r1 = c0 - hi
    mid = r1.astype(jnp.bfloat16).astype(jnp.float32)
    lo = (r1 - mid).astype(jnp.bfloat16).astype(jnp.float32)
    pieces = hi + pltpu.roll(mid, N_FOX_HEADS, axis=1) + pltpu.roll(lo, 2 * N_FOX_HEADS, axis=1)
    kb_ref[0] = (-pieces).astype(kb_ref.dtype)

    pc = proj[:, o_c:o_qkv]
    gate_b = pc[:, 0:D_CONV]
    z = pc[:, D_CONV:2 * D_CONV] * pc[:, 2 * D_CONV:3 * D_CONV]
    zbuf[CONV_HALO:CONV_HALO + tm, :] = z
    z1 = zbuf[CONV_HALO - 1:CONV_HALO - 1 + tm, :]
    z2 = zbuf[CONV_HALO - 2:CONV_HALO - 2 + tm, :]
    w = convw_ref[...]
    conv = w[0:1, :] * z2 + w[1:2, :] * z1 + w[2:3, :] * z
    ya_ref[0] = (gate_b * conv).astype(ya_ref.dtype)
    zbuf[0:CONV_HALO, :] = z[tm - CONV_HALO:tm, :]

    pqkv = proj[:, o_qkv:o_qkv + 3 * D_FOX]
    k_ref[0] = pqkv[:, D_FOX:2 * D_FOX].astype(k_ref.dtype)
    for r in range(tm // ATTN_BLOCK):
        rows = slice(r * ATTN_BLOCK, (r + 1) * ATTN_BLOCK)
        q_ref[0, r] = (pqkv[rows, 0:D_FOX] * (FOX_HEAD_DIM ** -0.5 * LOG2E)).T.astype(q_ref.dtype)
        v_ref[0, r] = pqkv[rows, 2 * D_FOX:3 * D_FOX].T.astype(v_ref.dtype)


def _mixer_in(x3d, layer, wall, bf, conv_w, sg, sb, w_sp, b_sp):
    bsz, seq, _ = x3d.shape
    tm = MIXER_TILE
    row_block = lambda width: pl.BlockSpec((1, tm, width), lambda b, j: (b, j, 0))
    slab_block = pl.BlockSpec((1, tm // ATTN_BLOCK, D_FOX, ATTN_BLOCK), lambda b, j: (b, j, 0, 0))
    bf16 = jnp.bfloat16
    params = (wall, bf, conv_w, sg, sb, w_sp, b_sp)
    return pl.pallas_call(
        _mixer_in_kernel,
        grid=(bsz, seq // tm),
        in_specs=[row_block(D_MODEL)] + [_layer_spec(p, layer) for p in params],
        out_specs=[
            row_block(D_CONV), row_block(D_SGU), slab_block, row_block(D_FOX), slab_block,
            row_block(LANES), row_block(LANES),
        ],
        out_shape=[
            jax.ShapeDtypeStruct((bsz, seq, D_CONV), bf16),
            jax.ShapeDtypeStruct((bsz, seq, D_SGU), bf16),
            jax.ShapeDtypeStruct((bsz, seq // ATTN_BLOCK, D_FOX, ATTN_BLOCK), bf16),
            jax.ShapeDtypeStruct((bsz, seq, D_FOX), bf16),
            jax.ShapeDtypeStruct((bsz, seq // ATTN_BLOCK, D_FOX, ATTN_BLOCK), bf16),
            jax.ShapeDtypeStruct((bsz, seq, LANES), jnp.float32),
            jax.ShapeDtypeStruct((bsz, seq, LANES), bf16),
        ],
        scratch_shapes=[
            pltpu.VMEM((CONV_HALO + tm, D_CONV), jnp.float32),
            pltpu.VMEM((1, LANES), jnp.float32),
        ],
        compiler_params=pltpu.CompilerParams(
            dimension_semantics=("arbitrary", "arbitrary"), vmem_limit_bytes=VMEM_LIMIT_BYTES),
        name="mixer_in",
    )(x3d, *params)


N_PAIRS = D_FOX // LANES


def _fox_attn_kernel(q_ref, k_ref, v_ref, cum_ref, kb_ref, o_ref, qs_sc, cq_sc, m_sc, l_sc, acc_sc, s_sc):
    i = pl.program_id(1)
    tq = q_ref.shape[3]
    tk = tq
    dim = lax.broadcasted_iota(jnp.int32, (LANES, tq), 0)
    first_head = dim < FOX_HEAD_DIM

    cum_t = cum_ref[0].T
    for p in range(N_PAIRS):
        q2t = q_ref[0, 0, p * LANES:(p + 1) * LANES, :]
        zero = jnp.zeros_like(q2t)
        heads = [jnp.where(first_head, q2t, zero), jnp.where(first_head, zero, q2t)]
        for hh in range(2):
            h = 2 * p + hh
            cq_sc[p, :, hh * tq:(hh + 1) * tq] = cum_t[h:h + 1, :]
            pick = (dim == h) | (dim == N_FOX_HEADS + h) | (dim == 2 * N_FOX_HEADS + h)
            qs_sc[p, 0:LANES, hh * tq:(hh + 1) * tq] = heads[hh]
            qs_sc[p, LANES:2 * LANES, hh * tq:(hh + 1) * tq] = jnp.where(pick, 1.0, 0.0).astype(qs_sc.dtype)
    m_sc[...] = jnp.full(m_sc.shape, -jnp.inf, jnp.float32)
    l_sc[...] = jnp.zeros(l_sc.shape, jnp.float32)
    acc_sc[...] = jnp.zeros(acc_sc.shape, jnp.float32)

    key = lax.broadcasted_iota(jnp.int32, (tk, 2 * tq), 0)
    qry = lax.broadcasted_iota(jnp.int32, (tk, 2 * tq), 1)
    qry = jnp.where(qry >= tq, qry - tq, qry)

    def step(j, diagonal):
        start = pl.multiple_of(j * tk, tk)

        def scores(p):
            k2 = jnp.concatenate(
                [k_ref[0, pl.ds(start, tk), p * LANES:(p + 1) * LANES], kb_ref[0, pl.ds(start, tk), :]], axis=1)
            s_sc[p % 2] = jnp.dot(k2, qs_sc[p], preferred_element_type=jnp.float32)

        scores(0)
        for p in range(N_PAIRS):
            if p + 1 < N_PAIRS:
                scores(p + 1)
            s = s_sc[p % 2]
            v2t = v_ref[0, j, p * LANES:(p + 1) * LANES, :]
            if diagonal:
                s = jnp.where(key <= qry, s, -jnp.inf)
            cq = cq_sc[p]
            m_old = m_sc[p]
            m_new = jnp.maximum(m_old, jnp.max(s, axis=0, keepdims=True) + cq)
            scale = jnp.exp2(m_old - m_new)
            probs = jnp.exp2(s - (m_new - cq))
            l_sc[p] = scale * l_sc[p] + jnp.sum(probs, axis=0, keepdims=True)
            pv = jnp.dot(v2t, probs.astype(jnp.bfloat16), preferred_element_type=jnp.float32)
            acc_sc[p] = scale * acc_sc[p] + pv
            m_sc[p] = m_new

    def body(j, carry):
        step(j, diagonal=False)
        return carry

    lax.fori_loop(0, i, body, 0)
    step(i, diagonal=True)

    for p in range(N_PAIRS):
        o = acc_sc[p] / l_sc[p]
        o = jnp.where(first_head, o[:, 0:tq], o[:, tq:2 * tq])
        o_ref[0, :, p * LANES:(p + 1) * LANES] = o.T.astype(o_ref.dtype)


def _fox_attn(q, k, v, cum, kb):
    bsz, seq, _ = k.shape
    tq = ATTN_BLOCK
    stat = pltpu.VMEM((N_PAIRS, 1, 2 * tq), jnp.float32)
    return pl.pallas_call(
        _fox_attn_kernel,
        grid=(bsz, seq // tq),
        in_specs=[
            pl.BlockSpec((1, 1, D_FOX, tq), lambda b, i: (b, i, 0, 0)),
            pl.BlockSpec((1, seq, D_FOX), lambda b, i: (b, 0, 0)),
            pl.BlockSpec((1, seq // tq, D_FOX, tq), lambda b, i: (b, 0, 0, 0)),
            pl.BlockSpec((1, tq, LANES), lambda b, i: (b, i, 0)),
            pl.BlockSpec((1, seq, LANES), lambda b, i: (b, 0, 0)),
        ],
        out_specs=pl.BlockSpec((1, tq, D_FOX), lambda b, i: (b, i, 0)),
        out_shape=jax.ShapeDtypeStruct((bsz, seq, D_FOX), jnp.bfloat16),
        scratch_shapes=[pltpu.VMEM((N_PAIRS, 2 * LANES, 2 * tq), jnp.bfloat16), stat, stat, stat,
                        pltpu.VMEM((N_PAIRS, LANES, 2 * tq), jnp.float32),
                        pltpu.VMEM((2, tq, 2 * tq), jnp.float32)],
        compiler_params=pltpu.CompilerParams(
            dimension_semantics=("arbitrary", "arbitrary"), vmem_limit_bytes=VMEM_LIMIT_BYTES),
        name="fox_attn",
    )(q, k, v, cum, kb)


def kernel(x, ln1_g, ln1_b, ffn1_w_up, ffn1_w_down, mix_w_in, fox_b_f, conv_w, sgu_ln_g, sgu_ln_b, sgu_w_s,
           sgu_b_s, mix_w_out, ln2_g, ln2_b, ffn2_w_up, ffn2_w_down, ln3_g, ln3_b):
    bsz, seq, d = x.shape
    assert d == D_MODEL and seq % MIXER_TILE == 0 and seq % ATTN_BLOCK == 0 and (bsz * seq) % ROW_TILE == 0
    bf16 = jnp.bfloat16
    rows = bsz * seq
    row_vecs = lambda p: p.reshape(p.shape[0], 1, -1)
    o_qkv = 3 * D_CONV
    o_f = o_qkv + 3 * D_FOX
    o_s = o_f + N_FOX_HEADS
    pad_f = LANES - N_FOX_HEADS

    w_up1, w_dn1 = ffn1_w_up.astype(bf16), ffn1_w_down.astype(bf16)
    w_up2, w_dn2 = ffn2_w_up.astype(bf16), ffn2_w_down.astype(bf16)
    wall = jnp.concatenate(
        [mix_w_in[:, :, o_s:o_s + 2 * D_SGU], jnp.pad(mix_w_in[:, :, o_f:o_s], ((0, 0), (0, 0), (0, pad_f))),
         mix_w_in[:, :, 0:o_f]], axis=2).astype(bf16)
    b_f = jnp.pad(fox_b_f, ((0, 0), (0, pad_f))).reshape(DEPTH, 1, LANES)
    b_sp = jnp.repeat(jnp.swapaxes(sgu_b_s, 1, 2), SGU_GROUP_DIM, axis=2)
    w_out = mix_w_out.astype(bf16)
    g1, b1, g2, b2, g3, b3 = (row_vecs(p) for p in (ln1_g, ln1_b, ln2_g, ln2_b, ln3_g, ln3_b))
    sg, sb = row_vecs(sgu_ln_g), row_vecs(sgu_ln_b)

    h = x.reshape(rows, d)
    for l in range(DEPTH):
        h = _ffn_ln(h, l, w_up1, w_dn1, g1, b1)
        ya, yc, q, k, v, cum, kb = _mixer_in(h.reshape(bsz, seq, d), l, wall, b_f, conv_w, sg, sb, sgu_w_s, b_sp)
        yb = _fox_attn(q, k, v, cum, kb)
        mixer = (ya.reshape(rows, D_CONV), yb.reshape(rows, D_FOX), yc.reshape(rows, D_SGU), w_out, g2, b2)
        h = _ffn_ln(h, l, w_up2, w_dn2, g3, b3, mixer=mixer)
    return h.reshape(bsz, seq, d)
```

```python
import functools

import jax
import jax.numpy as jnp
from jax import lax
from jax.experimental import pallas as pl
from jax.experimental.pallas import tpu as pltpu

D_MODEL = 1024
DEPTH = 2
D_CONV = 256
CONV_WIDTH = 3
D_FOX = 512
FOX_HEAD_DIM = 64
N_FOX_HEADS = 8
D_SGU = 256
N_SGU_GROUPS = 4
SGU_GROUP_DIM = 64
SGU_CHUNK = 128
D_FF = 2816
ALPHA = (2 * DEPTH) ** 0.25
LN_EPS = 1e-5
LOG2E = 1.4426950408889634

LANES = 128
CONV_HALO = 8
VMEM_LIMIT_BYTES = 56 * 1024 * 1024

ROW_TILE = 512
MIXER_TILE = 1024
FF_CHUNK = 256
ATTN_BLOCK = 512


def _layer_norm(h, g, b):
    mu = jnp.mean(h, axis=-1, keepdims=True)
    hc = h - mu
    var = jnp.mean(hc * hc, axis=-1, keepdims=True)
    return hc * lax.rsqrt(var + LN_EPS) * g + b


def _layer_spec(stacked, layer):
    zeros = (0,) * (stacked.ndim - 1)
    return pl.BlockSpec((None,) + stacked.shape[1:], lambda *_: (layer,) + zeros, pipeline_mode=pl.Buffered(1))


def _ffn_ln_kernel(*refs, mixed):
    if mixed:
        (x_ref, ya_ref, yb_ref, yc_ref, wo_ref, g2_ref, b2_ref,
         wup_ref, wdn_ref, g_ref, b_ref, o_ref, pre_sc, act_sc, h_sc) = refs
    else:
        x_ref, wup_ref, wdn_ref, g_ref, b_ref, o_ref, pre_sc, act_sc = refs
    lag = 1 if mixed else 0
    s = pl.program_id(0)
    last = pl.num_programs(0) - 1

    def mixer_out():
        y = jnp.concatenate([ya_ref[...], yb_ref[...], yc_ref[...]], axis=1)
        mix = jnp.dot(y, wo_ref[...], preferred_element_type=jnp.float32)
        return _layer_norm(ALPHA * x_ref[...] + mix, g2_ref[...], b2_ref[...])

    @pl.when(s == 0)
    def _():
        pre_sc[...] = jnp.zeros(pre_sc.shape, jnp.float32)
        if mixed:
            h_sc[...] = mixer_out()

    @pl.when((s >= lag) & (s < last))
    def _():
        done = _layer_norm(pre_sc[...], g_ref[...], b_ref[...])
        o_ref[...] = done
        h = h_sc[...] if mixed else x_ref[...]
        h_next = mixer_out() if mixed else None
        hb = h.astype(jnp.bfloat16)
        never = s < 0
        n_anchor = D_MODEL // FF_CHUNK
        for c in range(D_FF // FF_CHUNK):
            lo = c * FF_CHUNK
            gate = jnp.dot(hb, wup_ref[:, lo:lo + FF_CHUNK], preferred_element_type=jnp.float32)
            up = jnp.dot(hb, wup_ref[:, D_FF + lo:D_FF + lo + FF_CHUNK], preferred_element_type=jnp.float32)
            act = (jax.nn.silu(gate) * up).astype(jnp.bfloat16)
            if 1 <= c <= n_anchor:
                cols = slice((c - 1) * FF_CHUNK, c * FF_CHUNK)
                act = jnp.where(never, done[:, cols].astype(jnp.bfloat16), act)
            elif mixed and n_anchor + 2 <= c < 2 * n_anchor + 2:
                cols = slice((c - n_anchor - 2) * FF_CHUNK, (c - n_anchor - 1) * FF_CHUNK)
                act = jnp.where(never, h_next[:, cols].astype(jnp.bfloat16), act)
            act_sc[:, lo:lo + FF_CHUNK] = act
        acc = jnp.dot(act_sc[...], wdn_ref[...], preferred_element_type=jnp.float32)
        pre_sc[...] = ALPHA * h + 0.5 * acc
        if mixed:
            h_sc[...] = h_next

    @pl.when(s == last)
    def _():
        o_ref[...] = _layer_norm(pre_sc[...], g_ref[...], b_ref[...])


def _ffn_ln(x2d, layer, w_up, w_down, g, b, mixer=None):
    rows = x2d.shape[0]
    n_tiles = rows // ROW_TILE
    mixed = mixer is not None
    lag = 1 if mixed else 0
    in_tile = lambda s: jnp.minimum(s, n_tiles - 1)
    out_tile = lambda s: jnp.clip(s - lag - 1, 0, n_tiles - 1)
    row_block = lambda width: pl.BlockSpec((ROW_TILE, width), lambda s: (in_tile(s), 0))
    ffn_params = (w_up, w_down, g, b)
    in_specs = [row_block(D_MODEL)]
    operands = [x2d]
    scratch = [pltpu.VMEM((ROW_TILE, D_MODEL), jnp.float32), pltpu.VMEM((ROW_TILE, D_FF), jnp.bfloat16)]
    if mixed:
        ya, yb, yc, w_out, g2, b2 = mixer
        in_specs += [row_block(D_CONV), row_block(D_FOX), row_block(D_SGU)]
        in_specs += [_layer_spec(p, layer) for p in (w_out, g2, b2)]
        operands += [ya, yb, yc, w_out, g2, b2]
        scratch.append(pltpu.VMEM((ROW_TILE, D_MODEL), jnp.float32))
    in_specs += [_layer_spec(p, layer) for p in ffn_params]
    operands += list(ffn_params)
    return pl.pallas_call(
        functools.partial(_ffn_ln_kernel, mixed=mixed),
        grid=(n_tiles + 1 + lag,),
        in_specs=in_specs,
        out_specs=pl.BlockSpec((ROW_TILE, D_MODEL), lambda s: (out_tile(s), 0)),
        out_shape=jax.ShapeDtypeStruct((rows, D_MODEL), jnp.float32),
        scratch_shapes=scratch,
        compiler_params=pltpu.CompilerParams(
            dimension_semantics=("arbitrary",), vmem_limit_bytes=VMEM_LIMIT_BYTES),
        name="mix_ffn_ln" if mixed else "ffn_ln",
    )(*operands)


def _log_sigmoid(x):
    return -(jnp.maximum(-x, 0.0) + jnp.log1p(jnp.exp(-jnp.abs(x))))


def _cumsum_rows(y):
    n = y.shape[0]
    row = lax.broadcasted_iota(jnp.int32, y.shape, 0)
    shift = 1
    while shift < n:
        y = y + jnp.where(row >= shift, pltpu.roll(y, shift, axis=0), 0.0)
        shift *= 2
    return y


def _mixer_in_kernel(x_ref, wall_ref, bf_ref, convw_ref, sg_ref, sb_ref, wsp_ref, bsp_ref,
                     ya_ref, yc_ref, q_ref, k_ref, v_ref, cum_ref, kb_ref,
                     zbuf, fcarry):
    j = pl.program_id(1)
    tm = x_ref.shape[1]

    @pl.when(j == 0)
    def _():
        zbuf[0:CONV_HALO, :] = jnp.zeros((CONV_HALO, D_CONV), jnp.float32)
        fcarry[...] = jnp.zeros(fcarry.shape, jnp.float32)

    xb = x_ref[0].astype(jnp.bfloat16)
    proj = jnp.dot(xb, wall_ref[...], preferred_element_type=jnp.float32)
    o_f = 2 * D_SGU
    o_c = o_f + LANES
    o_qkv = o_c + 3 * D_CONV

    ps = proj[:, 0:o_f]
    u = jax.nn.gelu(ps[:, 0:D_SGU])
    vn = _layer_norm(jax.nn.gelu(ps[:, D_SGU:2 * D_SGU]), sg_ref[...], sb_ref[...]).astype(jnp.bfloat16)
    r = lax.broadcasted_iota(jnp.int32, (SGU_CHUNK, SGU_CHUNK), 0)
    c = lax.broadcasted_iota(jnp.int32, (SGU_CHUNK, SGU_CHUNK), 1)
    causal = c <= r
    lane = lax.broadcasted_iota(jnp.int32, (SGU_CHUNK, LANES), 1)
    first_group = lane < SGU_GROUP_DIM
    bias = bsp_ref[...]
    for half in range(D_SGU // LANES):
        w_pair = jnp.concatenate(
            [jnp.where(causal, wsp_ref[2 * half], 0.0), jnp.where(causal, wsp_ref[2 * half + 1], 0.0)],
            axis=0).astype(jnp.bfloat16)
        for n in range(tm // SGU_CHUNK):
            rows = slice(n * SGU_CHUNK, (n + 1) * SGU_CHUNK)
            cols = slice(half * LANES, (half + 1) * LANES)
            both = jnp.dot(w_pair, vn[rows, cols], preferred_element_type=jnp.float32)
            gated = jnp.where(first_group, both[0:SGU_CHUNK], both[SGU_CHUNK:2 * SGU_CHUNK]) + bias[:, cols]
            yc_ref[0, rows, cols] = (u[rows, cols] * gated).astype(yc_ref.dtype)

    f_logit = proj[:, o_f:o_c] + bf_ref[...]
    cum = _cumsum_rows(_log_sigmoid(f_logit)) + fcarry[...]
    fcarry[...] = cum[tm - 1:tm, :]
    cum = cum * LOG2E
    cum_ref[0] = cum
    lane = lax.broadcasted_iota(jnp.int32, cum.shape, 1)
    c0 = jnp.where(lane < N_FOX_HEADS, cum, 0.0)
    hi = c0.astype(jnp.bfloat16).astype(jnp.float32)
    r1 = c0 - hi
    mid = r1.astype(jnp.bfloat16).astype(jnp.float32)
    lo = (r1 - mid).astype(jnp.bfloat16).astype(jnp.float32)
    pieces = hi + pltpu.roll(mid, N_FOX_HEADS, axis=1) + pltpu.roll(lo, 2 * N_FOX_HEADS, axis=1)
    kb_ref[0] = (-pieces).astype(kb_ref.dtype)

    pc = proj[:, o_c:o_qkv]
    gate_b = pc[:, 0:D_CONV]
    z = pc[:, D_CONV:2 * D_CONV] * pc[:, 2 * D_CONV:3 * D_CONV]
    zbuf[CONV_HALO:CONV_HALO + tm, :] = z
    z1 = zbuf[CONV_HALO - 1:CONV_HALO - 1 + tm, :]
    z2 = zbuf[CONV_HALO - 2:CONV_HALO - 2 + tm, :]
    w = convw_ref[...]
    conv = w[0:1, :] * z2 + w[1:2, :] * z1 + w[2:3, :] * z
    ya_ref[0] = (gate_b * conv).astype(ya_ref.dtype)
    zbuf[0:CONV_HALO, :] = z[tm - CONV_HALO:tm, :]

    pqkv = proj[:, o_qkv:o_qkv + 3 * D_FOX]
    k_ref[0] = pqkv[:, D_FOX:2 * D_FOX].astype(k_ref.dtype)
    for r in range(tm // ATTN_BLOCK):
        rows = slice(r * ATTN_BLOCK, (r + 1) * ATTN_BLOCK)
        q_ref[0, r] = (pqkv[rows, 0:D_FOX] * (FOX_HEAD_DIM ** -0.5 * LOG2E)).T.astype(q_ref.dtype)
        v_ref[0, r] = pqkv[rows, 2 * D_FOX:3 * D_FOX].T.astype(v_ref.dtype)


def _mixer_in(x3d, layer, wall, bf, conv_w, sg, sb, w_sp, b_sp):
    bsz, seq, _ = x3d.shape
    tm = MIXER_TILE
    row_block = lambda width: pl.BlockSpec((1, tm, width), lambda b, j: (b, j, 0))
    slab_block = pl.BlockSpec((1, tm // ATTN_BLOCK, D_FOX, ATTN_BLOCK), lambda b, j: (b, j, 0, 0))
    bf16 = jnp.bfloat16
    params = (wall, bf, conv_w, sg, sb, w_sp, b_sp)
    return pl.pallas_call(
        _mixer_in_kernel,
        grid=(bsz, seq // tm),
        in_specs=[row_block(D_MODEL)] + [_layer_spec(p, layer) for p in params],
        out_specs=[
            row_block(D_CONV), row_block(D_SGU), slab_block, row_block(D_FOX), slab_block,
            row_block(LANES), row_block(LANES),
        ],
        out_shape=[
            jax.ShapeDtypeStruct((bsz, seq, D_CONV), bf16),
            jax.ShapeDtypeStruct((bsz, seq, D_SGU), bf16),
            jax.ShapeDtypeStruct((bsz, seq // ATTN_BLOCK, D_FOX, ATTN_BLOCK), bf16),
            jax.ShapeDtypeStruct((bsz, seq, D_FOX), bf16),
            jax.ShapeDtypeStruct((bsz, seq // ATTN_BLOCK, D_FOX, ATTN_BLOCK), bf16),
            jax.ShapeDtypeStruct((bsz, seq, LANES), jnp.float32),
            jax.ShapeDtypeStruct((bsz, seq, LANES), bf16),
        ],
        scratch_shapes=[
            pltpu.VMEM((CONV_HALO + tm, D_CONV), jnp.float32),
            pltpu.VMEM((1, LANES), jnp.float32),
        ],
        compiler_params=pltpu.CompilerParams(
            dimension_semantics=("arbitrary", "arbitrary"), vmem_limit_bytes=VMEM_LIMIT_BYTES),
        name="mixer_in",
    )(x3d, *params)


N_PAIRS = D_FOX // LANES


def _fox_attn_kernel(q_ref, k_ref, v_ref, cum_ref, kb_ref, o_ref, qs_sc, cq_sc, m_sc, l_sc, acc_sc, s_sc):
    i = pl.program_id(1)
    tq = q_ref.shape[3]
    tk = tq
    dim = lax.broadcasted_iota(jnp.int32, (LANES, tq), 0)
    first_head = dim < FOX_HEAD_DIM

    cum_t = cum_ref[0].T
    for p in range(N_PAIRS):
        q2t = q_ref[0, 0, p * LANES:(p + 1) * LANES, :]
        zero = jnp.zeros_like(q2t)
        heads = [jnp.where(first_head, q2t, zero), jnp.where(first_head, zero, q2t)]
        for hh in range(2):
            h = 2 * p + hh
            cq_sc[p, :, hh * tq:(hh + 1) * tq] = cum_t[h:h + 1, :]
            pick = (dim == h) | (dim == N_FOX_HEADS + h) | (dim == 2 * N_FOX_HEADS + h)
            qs_sc[p, 0:LANES, hh * tq:(hh + 1) * tq] = heads[hh]
            qs_sc[p, LANES:2 * LANES, hh * tq:(hh + 1) * tq] = jnp.where(pick, 1.0, 0.0).astype(qs_sc.dtype)
    m_sc[...] = jnp.full(m_sc.shape, -jnp.inf, jnp.float32)
    l_sc[...] = jnp.zeros(l_sc.shape, jnp.float32)
    acc_sc[...] = jnp.zeros(acc_sc.shape, jnp.float32)

    key = lax.broadcasted_iota(jnp.int32, (tk, 2 * tq), 0)
    qry = lax.broadcasted_iota(jnp.int32, (tk, 2 * tq), 1)
    qry = jnp.where(qry >= tq, qry - tq, qry)

    causal = key <= qry

    def scores(j, p):
        start = pl.multiple_of(j * tk, tk)
        k2 = jnp.concatenate(
            [k_ref[0, pl.ds(start, tk), p * LANES:(p + 1) * LANES], kb_ref[0, pl.ds(start, tk), :]], axis=1)
        s_sc[p % 2] = jnp.dot(k2, qs_sc[p], preferred_element_type=jnp.float32)

    def step(j, diagonal):
        for p in range(N_PAIRS):
            if p + 1 < N_PAIRS:
                scores(j, p + 1)
            elif not diagonal:
                scores(j + 1, 0)
            s = s_sc[p % 2]
            if diagonal:
                s = jnp.where(causal, s, -jnp.inf)
            v2t = v_ref[0, j, p * LANES:(p + 1) * LANES, :]
            cq = cq_sc[p]
            m_old = m_sc[p]
            m_new = jnp.maximum(m_old, jnp.max(s, axis=0, keepdims=True) + cq)
            scale = jnp.exp2(m_old - m_new)
            probs = jnp.exp2(s - (m_new - cq))
            l_sc[p] = scale * l_sc[p] + jnp.sum(probs, axis=0, keepdims=True)
            pv = jnp.dot(v2t, probs.astype(jnp.bfloat16), preferred_element_type=jnp.float32)
            acc_sc[p] = scale * acc_sc[p] + pv
            m_sc[p] = m_new

    def body(j, carry):
        step(j, diagonal=False)
        return carry

    scores(0, 0)
    lax.fori_loop(0, i, body, 0)
    step(i, diagonal=True)

    for p in range(N_PAIRS):
        o = acc_sc[p] / l_sc[p]
        o = jnp.where(first_head, o[:, 0:tq], o[:, tq:2 * tq])
        o_ref[0, :, p * LANES:(p + 1) * LANES] = o.T.astype(o_ref.dtype)


def _fox_attn(q, k, v, cum, kb):
    bsz, seq, _ = k.shape
    tq = ATTN_BLOCK
    stat = pltpu.VMEM((N_PAIRS, 1, 2 * tq), jnp.float32)
    return pl.pallas_call(
        _fox_attn_kernel,
        grid=(bsz, seq // tq),
        in_specs=[
            pl.BlockSpec((1, 1, D_FOX, tq), lambda b, i: (b, i, 0, 0)),
            pl.BlockSpec((1, seq, D_FOX), lambda b, i: (b, 0, 0)),
            pl.BlockSpec((1, seq // tq, D_FOX, tq), lambda b, i: (b, 0, 0, 0)),
            pl.BlockSpec((1, tq, LANES), lambda b, i: (b, i, 0)),
            pl.BlockSpec((1, seq, LANES), lambda b, i: (b, 0, 0)),
        ],
        out_specs=pl.BlockSpec((1, tq, D_FOX), lambda b, i: (b, i, 0)),
        out_shape=jax.ShapeDtypeStruct((bsz, seq, D_FOX), jnp.bfloat16),
        scratch_shapes=[pltpu.VMEM((N_PAIRS, 2 * LANES, 2 * tq), jnp.bfloat16), stat, stat, stat,
                        pltpu.VMEM((N_PAIRS, LANES, 2 * tq), jnp.float32),
                        pltpu.VMEM((2, tq, 2 * tq), jnp.float32)],
        compiler_params=pltpu.CompilerParams(
            dimension_semantics=("arbitrary", "arbitrary"), vmem_limit_bytes=VMEM_LIMIT_BYTES),
        name="fox_attn",
    )(q, k, v, cum, kb)


def kernel(x, ln1_g, ln1_b, ffn1_w_up, ffn1_w_down, mix_w_in, fox_b_f, conv_w, sgu_ln_g, sgu_ln_b, sgu_w_s,
           sgu_b_s, mix_w_out, ln2_g, ln2_b, ffn2_w_up, ffn2_w_down, ln3_g, ln3_b):
    bsz, seq, d = x.shape
    assert d == D_MODEL and seq % MIXER_TILE == 0 and seq % ATTN_BLOCK == 0 and (bsz * seq) % ROW_TILE == 0
    bf16 = jnp.bfloat16
    rows = bsz * seq
    row_vecs = lambda p: p.reshape(p.shape[0], 1, -1)
    o_qkv = 3 * D_CONV
    o_f = o_qkv + 3 * D_FOX
    o_s = o_f + N_FOX_HEADS
    pad_f = LANES - N_FOX_HEADS

    w_up1, w_dn1 = ffn1_w_up.astype(bf16), ffn1_w_down.astype(bf16)
    w_up2, w_dn2 = ffn2_w_up.astype(bf16), ffn2_w_down.astype(bf16)
    wall = jnp.concatenate(
        [mix_w_in[:, :, o_s:o_s + 2 * D_SGU], jnp.pad(mix_w_in[:, :, o_f:o_s], ((0, 0), (0, 0), (0, pad_f))),
         mix_w_in[:, :, 0:o_f]], axis=2).astype(bf16)
    b_f = jnp.pad(fox_b_f, ((0, 0), (0, pad_f))).reshape(DEPTH, 1, LANES)
    b_sp = jnp.repeat(jnp.swapaxes(sgu_b_s, 1, 2), SGU_GROUP_DIM, axis=2)
    w_out = mix_w_out.astype(bf16)
    g1, b1, g2, b2, g3, b3 = (row_vecs(p) for p in (ln1_g, ln1_b, ln2_g, ln2_b, ln3_g, ln3_b))
    sg, sb = row_vecs(sgu_ln_g), row_vecs(sgu_ln_b)

    h = x.reshape(rows, d)
    for l in range(DEPTH):
        h = _ffn_ln(h, l, w_up1, w_dn1, g1, b1)
        ya, yc, q, k, v, cum, kb = _mixer_in(h.reshape(bsz, seq, d), l, wall, b_f, conv_w, sg, sb, sgu_w_s, b_sp)
        yb = _fox_attn(q, k, v, cum, kb)
        mixer = (ya.reshape(rows, D_CONV), yb.reshape(rows, D_FOX), yc.reshape(rows, D_SGU), w_out, g2, b2)
        h = _ffn_ln(h, l, w_up2, w_dn2, g3, b3, mixer=mixer)
    return h.reshape(bsz, seq, d)
```
